```python
import jax
import jax.numpy as jnp
from jax import lax
import numpy as np

D_MODEL = 2048
BATCH = 4
SEQ = 4096
DEPTH = 4

GRID_W = 64
CTX_LEN = 256
N_MIXERS = 3
RET_HEADS = 8
RET_QK_DIM = D_MODEL // RET_HEADS
RET_V_DIM = 2 * RET_QK_DIM
RET_CHUNK = 128
ATT_HEAD_DIM = 128
ATT_HEADS = D_MODEL // ATT_HEAD_DIM
ATT_KV_HEADS = 4
ATT_GROUPS = ATT_HEADS // ATT_KV_HEADS
Q_BLOCK = 128
ROPE_BASE = 10000.0
CONV_WIDTH = 31
FFN_DIM = ((8 * D_MODEL // 3 + 255) // 256) * 256
FFN_CONV_WIDTH = 3
N_MOD = 6
RMS_EPS = 1e-6
LN_EPS = 1e-5

kernel_name = 'hybrid_retention_gqa_conformer_dit'


def rmsnorm(x, w):
    xf = x.astype(jnp.float32)
    y = xf * lax.rsqrt(jnp.mean(xf * xf, axis=-1, keepdims=True) + RMS_EPS)
    return (y * w.astype(jnp.float32)).astype(x.dtype)


def layernorm(x, w, b):
    xf = x.astype(jnp.float32)
    mu = jnp.mean(xf, axis=-1, keepdims=True)
    var = jnp.mean(jnp.square(xf - mu), axis=-1, keepdims=True)
    y = (xf - mu) * lax.rsqrt(var + LN_EPS)
    return (y * w.astype(jnp.float32) + b.astype(jnp.float32)).astype(x.dtype)


def depthwise_conv(x, w, b):
    y = lax.conv_general_dilated(x, w[:, None, :].astype(x.dtype), window_strides=(1,), padding='SAME',
                                 dimension_numbers=('NWC', 'WIO', 'NWC'), feature_group_count=x.shape[-1])
    return y + b.astype(x.dtype)


def axial_rope(x, row, col):
    n, hd = x.shape[1], x.shape[-1]
    half = hd // 2
    nf = half // 2
    inv_freq = ROPE_BASE ** (-jnp.arange(nf, dtype=jnp.float32) / nf)
    shape = (1, n) + (1,) * (x.ndim - 3) + (nf,)

    def rot(xa, pos):
        ang = pos.astype(jnp.float32)[:, None] * inv_freq[None, :]
        cos = jnp.cos(ang).reshape(shape)
        sin = jnp.sin(ang).reshape(shape)
        x1 = xa[..., :nf].astype(jnp.float32)
        x2 = xa[..., nf:].astype(jnp.float32)
        return jnp.concatenate([x1 * cos - x2 * sin, x2 * cos + x1 * sin], axis=-1)

    out = jnp.concatenate([rot(x[..., :half], row), rot(x[..., half:], col)], axis=-1)
    return out.astype(x.dtype)


def sdpa_block(q, k, v):
    s = jnp.einsum('bqkgd,bnkd->bkgqn', q, k).astype(jnp.float32) * (q.shape[-1] ** -0.5)
    p = jax.nn.softmax(s, axis=-1).astype(v.dtype)
    return jnp.einsum('bkgqn,bnkd->bqkgd', p, v)


def attention_mixer(hl, hc, wq, wkv, q_gain, k_gain, wo, with_ctx):
    B, S, _ = hl.shape
    rows = S // GRID_W
    row = jnp.repeat(jnp.arange(rows), GRID_W)
    col = jnp.tile(jnp.arange(GRID_W), rows)

    def project(h):
        n = h.shape[1]
        q = rmsnorm((h @ wq).reshape(B, n, ATT_KV_HEADS, ATT_GROUPS, ATT_HEAD_DIM), q_gain)
        kv = (h @ wkv).reshape(B, n, 2, ATT_KV_HEADS, ATT_HEAD_DIM)
        return q, rmsnorm(kv[:, :, 0], k_gain), kv[:, :, 1]

    ql, kl, vl = project(hl)
    qc, kc, vc = project(hc)
    ql = axial_rope(ql, row, col)
    kl = axial_rope(kl, row, col)
    k_all = jnp.concatenate([kc, kl], axis=1)
    v_all = jnp.concatenate([vc, vl], axis=1)
    nb = S // Q_BLOCK
    qb = ql.reshape(B, nb, Q_BLOCK, ATT_KV_HEADS, ATT_GROUPS, ATT_HEAD_DIM).swapaxes(0, 1)
    yl = lax.map(lambda qi: sdpa_block(qi, k_all, v_all), qb)
    yl = yl.swapaxes(0, 1).reshape(B, S, D_MODEL) @ wo
    yc = sdpa_block(qc, kc, vc).reshape(B, hc.shape[1], D_MODEL) @ wo if with_ctx else None
    return yl, yc


def retention_chunkwise(q, k, v, log_gamma, state0):
    B, H, N, dk = q.shape
    dv = v.shape[-1]
    C = RET_CHUNK
    nc = N // C
    pos = jnp.arange(C, dtype=jnp.float32)
    diff = pos[:, None] - pos[None, :]
    lg = log_gamma[:, None]
    intra = jnp.where(diff >= 0, jnp.exp(log_gamma[:, None, None] * jnp.maximum(diff, 0.0)), 0.0)
    q_dec = jnp.exp(lg * (pos + 1.0))[None, :, :, None]
    k_dec = jnp.exp(lg * (C - 1.0 - pos))[None, :, :, None]
    chunk_dec = jnp.exp(log_gamma * C)[None, :, None, None]

    def split(t):
        return t.reshape(B, H, nc, C, t.shape[-1]).transpose(2, 0, 1, 3, 4)

    def step(state, inp):
        qc, kc, vc = inp
        scores = jnp.einsum('bhnk,bhmk->bhnm', qc, kc) * intra
        out = jnp.einsum('bhnm,bhmv->bhnv', scores, vc) + jnp.einsum('bhnk,bhkv->bhnv', qc, state) * q_dec
        state = state * chunk_dec + jnp.einsum('bhmk,bhmv->bhkv', kc * k_dec, vc)
        return state, out

    _, outs = lax.scan(step, state0, (split(q), split(k), split(v)))
    return outs.transpose(1, 2, 0, 3, 4).reshape(B, H, N, dv)


def retention_bidirectional(q, k, v, log_gamma, state_f, state_b):
    fwd = retention_chunkwise(q, k, v, log_gamma[0], state_f)
    rev = lambda t: jnp.flip(t, axis=2)
    bwd = rev(retention_chunkwise(rev(q), rev(k), rev(v), log_gamma[1], state_b))
    return fwd + bwd


def retention_output(y, h, wg, wo, gn_w):
    mu = jnp.mean(y, axis=-1, keepdims=True)
    var = jnp.mean(jnp.square(y - mu), axis=-1, keepdims=True)
    y = (y - mu) * lax.rsqrt(var + LN_EPS)
    B, H, N, dv = y.shape
    y = y.transpose(0, 2, 1, 3).reshape(B, N, H * dv) * gn_w.astype(jnp.float32)
    return (jax.nn.silu(h @ wg) * y.astype(h.dtype)) @ wo


def retention_mixer(hl, hc, wq, wk, wv, wg, wo, decay, gn_w, with_ctx):
    log_gamma = -jnp.exp(decay.astype(jnp.float32))
    k_scale = RET_QK_DIM ** -0.5

    def heads(h, w, dh):
        B, n, _ = h.shape
        return (h @ w).reshape(B, n, RET_HEADS, dh).transpose(0, 2, 1, 3).astype(jnp.float32)

    kc = heads(hc, wk, RET_QK_DIM) * k_scale
    vc = heads(hc, wv, RET_V_DIM)
    lc = hc.shape[1]
    m = jnp.arange(lc, dtype=jnp.float32)
    w_f = jnp.exp(log_gamma[0][:, None] * (lc - 1.0 - m))[None, :, :, None]
    w_b = jnp.exp(log_gamma[1][:, None] * m)[None, :, :, None]
    state_f = jnp.einsum('bhmk,bhmv->bhkv', kc * w_f, vc)
    state_b = jnp.einsum('bhmk,bhmv->bhkv', kc * w_b, vc)
    ql = heads(hl, wq, RET_QK_DIM)
    kl = heads(hl, wk, RET_QK_DIM) * k_scale
    vl = heads(hl, wv, RET_V_DIM)
    yl = retention_output(retention_bidirectional(ql, kl, vl, log_gamma, state_f, state_b), hl, wg, wo, gn_w)
    yc = None
    if with_ctx:
        zero = jnp.zeros_like(state_f)
        qc = heads(hc, wq, RET_QK_DIM)
        yc = retention_output(retention_bidirectional(qc, kc, vc, log_gamma, zero, zero), hc, wg, wo, gn_w)
    return yl, yc


def conformer_conv(h, w1, b1, dw, dw_b, ln_w, ln_b, w2, b2):
    a = h @ w1 + b1
    a = a[..., :D_MODEL] * jax.nn.sigmoid(a[..., D_MODEL:])
    a = layernorm(depthwise_conv(a, dw, dw_b), ln_w, ln_b)
    return jax.nn.silu(a) @ w2 + b2


def conv_ffn(h, w_gate, w_up, dw, dw_b, w_down):
    g = depthwise_conv(h @ w_gate, dw, dw_b)
    return (jax.nn.silu(g) * (h @ w_up)) @ w_down


def setup_inputs(seed: int = 0) -> dict:
    key = jax.random.key(seed)
    ks = iter(jax.random.split(key, 48))
    n_a = (DEPTH + N_MIXERS - 1) // N_MIXERS
    n_b = (DEPTH + N_MIXERS - 2) // N_MIXERS
    n_c = DEPTH // N_MIXERS
    D = D_MODEL
    HV = RET_HEADS * RET_V_DIM

    def normal(shape):
        return jax.random.normal(next(ks), shape, jnp.float32)

    def dense(shape, fan_in, g=1.0):
        return normal(shape) * (g * fan_in ** -0.5)

    def gain(shape):
        return 1.0 + 0.02 * normal(shape)

    def bias(shape):
        return 0.02 * normal(shape)

    base_decay = jnp.log(-jnp.log1p(-jnp.exp2(-5.0 - jnp.arange(RET_HEADS, dtype=jnp.float32))))
    return {
        'x': normal((BATCH, SEQ, D)),
        'c': normal((BATCH, D)),
        'ctx': normal((BATCH, CTX_LEN, D)),
        'c_ctx': normal((D,)),
        'mod_w': dense((DEPTH, D, N_MOD * D), D, 0.5),
        'mod_b': bias((DEPTH, N_MOD * D)),
        'norm1_w': gain((DEPTH, D)),
        'norm2_w': gain((DEPTH, D)),
        'ffn_w_gate': dense((DEPTH, D, FFN_DIM), D),
        'ffn_w_up': dense((DEPTH, D, FFN_DIM), D),
        'ffn_dw': dense((DEPTH, FFN_CONV_WIDTH, FFN_DIM), FFN_CONV_WIDTH),
        'ffn_dw_b': bias((DEPTH, FFN_DIM)),
        'ffn_w_down': dense((DEPTH, FFN_DIM, D), FFN_DIM),
        'ret_wq': dense((n_a, D, RET_HEADS * RET_QK_DIM), D),
        'ret_wk': dense((n_a, D, RET_HEADS * RET_QK_DIM), D),
        'ret_wv': dense((n_a, D, HV), D),
        'ret_wg': dense((n_a, D, HV), D),
        'ret_wo': dense((n_a, HV, D), HV),
        'ret_decay': base_decay + 0.1 * normal((n_a, 2, RET_HEADS)),
        'ret_gn_w': gain((n_a, HV)),
        'att_wq': dense((n_b, D, ATT_HEADS * ATT_HEAD_DIM), D),
        'att_wkv': dense((n_b, D, 2 * ATT_KV_HEADS * ATT_HEAD_DIM), D),
        'att_q_gain': gain((n_b, ATT_HEAD_DIM)),
        'att_k_gain': gain((n_b, ATT_HEAD_DIM)),
        'att_wo': dense((n_b, ATT_HEADS * ATT_HEAD_DIM, D), D),
        'cnv_w1': dense((n_c, D, 2 * D), D),
        'cnv_b1': bias((n_c, 2 * D)),
        'cnv_dw': dense((n_c, CONV_WIDTH, D), CONV_WIDTH),
        'cnv_dw_b': bias((n_c, D)),
        'cnv_ln_w': gain((n_c, D)),
        'cnv_ln_b': bias((n_c, D)),
        'cnv_w2': dense((n_c, D, D), D),
        'cnv_b2': bias((n_c, D)),
    }


def reference(x, c, ctx, c_ctx, mod_w, mod_b, norm1_w, norm2_w, ffn_w_gate, ffn_w_up, ffn_dw, ffn_dw_b,
              ffn_w_down, ret_wq, ret_wk, ret_wv, ret_wg, ret_wo, ret_decay, ret_gn_w, att_wq, att_wkv,
              att_q_gain, att_k_gain, att_wo, cnv_w1, cnv_b1, cnv_dw, cnv_dw_b, cnv_ln_w, cnv_ln_b, cnv_w2,
              cnv_b2):
    xl, xc = x, ctx
    silu_c = jax.nn.silu(c)
    silu_cc = jax.nn.silu(c_ctx)
    for i in range(DEPTH):
        with_ctx = i < DEPTH - 1
        mod_l = (silu_c @ mod_w[i] + mod_b[i])[:, None, :]
        mod_c = silu_cc @ mod_w[i] + mod_b[i]
        sh1, sc1, g1, sh2, sc2, g2 = jnp.split(mod_l, N_MOD, axis=-1)
        csh1, csc1, cg1, csh2, csc2, cg2 = jnp.split(mod_c, N_MOD, axis=-1)
        hl = rmsnorm(xl, norm1_w[i]) * (1.0 + sc1) + sh1
        hc = rmsnorm(xc, norm1_w[i]) * (1.0 + csc1) + csh1
        kind, j = i % N_MIXERS, i // N_MIXERS
        if kind == 0:
            yl, yc = retention_mixer(hl, hc, ret_wq[j], ret_wk[j], ret_wv[j], ret_wg[j], ret_wo[j],
                                     ret_decay[j], ret_gn_w[j], with_ctx)
        elif kind == 1:
            yl, yc = attention_mixer(hl, hc, att_wq[j], att_wkv[j], att_q_gain[j], att_k_gain[j], att_wo[j],
                                     with_ctx)
        else:
            cp = (cnv_w1[j], cnv_b1[j], cnv_dw[j], cnv_dw_b[j], cnv_ln_w[j], cnv_ln_b[j], cnv_w2[j], cnv_b2[j])
            yl = conformer_conv(hl, *cp)
            yc = conformer_conv(hc, *cp) if with_ctx else None
        fp = (ffn_w_gate[i], ffn_w_up[i], ffn_dw[i], ffn_dw_b[i], ffn_w_down[i])
        xl = xl + g1 * yl
        xl = xl + g2 * conv_ffn(rmsnorm(xl, norm2_w[i]) * (1.0 + sc2) + sh2, *fp)
        if with_ctx:
            xc = xc + cg1 * yc
            xc = xc + cg2 * conv_ffn(rmsnorm(xc, norm2_w[i]) * (1.0 + csc2) + csh2, *fp)
    return xl
```

```python
import functools

import jax
import jax.numpy as jnp
from jax import lax
from jax.experimental import pallas as pl
from jax.experimental.pallas import tpu as pltpu

D_MODEL = 2048
BATCH = 4
SEQ = 4096
DEPTH = 4
GRID_W = 64
CTX_LEN = 256
N_MIXERS = 3
RET_HEADS = 8
RET_QK_DIM = D_MODEL // RET_HEADS
RET_V_DIM = 2 * RET_QK_DIM
RET_CHUNK = 128
ATT_HEAD_DIM = 128
ATT_HEADS = D_MODEL // ATT_HEAD_DIM
ATT_KV_HEADS = 4
ATT_GROUPS = ATT_HEADS // ATT_KV_HEADS
ROPE_BASE = 10000.0
CONV_WIDTH = 31
FFN_DIM = ((8 * D_MODEL // 3 + 255) // 256) * 256
FFN_CONV_WIDTH = 3
N_MOD = 6
RMS_EPS = 1e-6
LN_EPS = 1e-5

BF16 = jnp.bfloat16
F32 = jnp.float32

LANES = 128
SUBLANES = 8
VMEM_LIMIT_BYTES = 48 * 1024 * 1024

ROW_TILE = 512
COL_TILE = 1024
FUSED_COL_TILE = 512
ATT_Q_TILE = 256
CONV_ROW_TILE = 128
CONV_HALO = 16
CONV_COL_CHUNK = 256


def _tiles(extent, tile):
    assert extent % tile == 0, (extent, tile)
    return extent // tile


def _dot(a, b):
    return jnp.dot(a, b, preferred_element_type=F32)


def _dot_nt(a, b):
    return lax.dot_general(a, b, (((1,), (1,)), ((), ())), preferred_element_type=F32)


def _dot_tn(a, b):
    return lax.dot_general(a, b, (((0,), (0,)), ((), ())), preferred_element_type=F32)


def _sigmoid(x):
    return 1.0 / (1.0 + jnp.exp(-x))


def _silu(x):
    return x * _sigmoid(x)


def _modnorm(x, nw, sc, sh):
    ms = jnp.mean(x * x, axis=-1, keepdims=True)
    y = x * lax.rsqrt(ms + RMS_EPS)
    return (y * nw) * (1.0 + sc) + sh


def _params(*semantics):
    return pltpu.CompilerParams(dimension_semantics=semantics, vmem_limit_bytes=VMEM_LIMIT_BYTES)


def _mod_spec(layer, k, rows_per_sample):
    if rows_per_sample is None:
        index = lambda i, j: (layer, k, BATCH, 0, 0)
    else:
        tiles = _tiles(rows_per_sample, ROW_TILE)
        index = lambda i, j: (layer, k, i // tiles, 0, 0)
    return pl.BlockSpec((None, None, None, 1, D_MODEL), index)


def _row_spec(i, j):
    return (i, 0)


def _const_spec(i, j):
    return (0, 0)


def _mod_body(c_ref, w_ref, b_ref, o_ref):
    c = c_ref[...]
    o_ref[...] = _dot(_silu(c).astype(BF16), w_ref[...].astype(BF16)) + b_ref[...]


def _modulation(cin, mod_w, mod_b):
    n = N_MOD * D_MODEL
    tn = COL_TILE
    return pl.pallas_call(
        _mod_body,
        out_shape=jax.ShapeDtypeStruct((DEPTH, SUBLANES, n), F32),
        grid=(DEPTH, _tiles(n, tn)),
        in_specs=[
            pl.BlockSpec((SUBLANES, D_MODEL), lambda l, j: (0, 0)),
            pl.BlockSpec((None, D_MODEL, tn), lambda l, j: (l, 0, j)),
            pl.BlockSpec((None, 1, tn), lambda l, j: (l, 0, j)),
        ],
        out_specs=pl.BlockSpec((None, SUBLANES, tn), lambda l, j: (l, 0, j)),
        compiler_params=_params("parallel", "parallel"),
        name="modulation",
    )(cin, mod_w, mod_b.reshape(DEPTH, 1, n))


def _fill_h(x_ref, nw_ref, sc_ref, sh_ref, h_ref):
    @pl.when(pl.program_id(1) == 0)
    def _():
        h_ref[...] = _modnorm(x_ref[...], nw_ref[...], sc_ref[...], sh_ref[...]).astype(BF16)


def _proj_body(x_ref, nw_ref, sc_ref, sh_ref, w_ref, o_ref, h_ref):
    _fill_h(x_ref, nw_ref, sc_ref, sh_ref, h_ref)
    o_ref[...] = _dot(h_ref[...], w_ref[...]).astype(o_ref.dtype)


def _glu_body(x_ref, nw_ref, sc_ref, sh_ref, wa_ref, wb_ref, ba_ref, bb_ref, o_ref, h_ref):
    _fill_h(x_ref, nw_ref, sc_ref, sh_ref, h_ref)
    h = h_ref[...]
    a = _dot(h, wa_ref[...]) + ba_ref[...]
    b = _dot(h, wb_ref[...]) + bb_ref[...]
    o_ref[...] = (a * _sigmoid(b)).astype(o_ref.dtype)


def _headnorm_body(x_ref, nw_ref, sc_ref, sh_ref, w_ref, gain_ref, *rest, n_norm, scale, rope):
    if rope:
        cos_ref, sin_ref, o_ref, h_ref = rest
    else:
        o_ref, h_ref = rest
    _fill_h(x_ref, nw_ref, sc_ref, sh_ref, h_ref)
    acc = _dot(h_ref[...], w_ref[...])
    hd = ATT_HEAD_DIM
    for head in range(_tiles(n_norm, hd)):
        a = acc[:, head * hd:(head + 1) * hd]
        ms = jnp.mean(a * a, axis=-1, keepdims=True)
        a = (a * lax.rsqrt(ms + RMS_EPS)) * gain_ref[...]
        if rope:
            lane = lax.broadcasted_iota(jnp.int32, a.shape, 1)
            quarter = hd // 4
            partner = jnp.where((lane & (2 * quarter - 1)) < quarter,
                                pltpu.roll(a, hd - quarter, 1), pltpu.roll(a, quarter, 1))
            a = a * cos_ref[...] + partner * sin_ref[...]
        if scale != 1.0:
            a = a * scale
        o_ref[:, head * hd:(head + 1) * hd] = a.astype(o_ref.dtype)
    if n_norm < acc.shape[1]:
        o_ref[:, n_norm:] = acc[:, n_norm:].astype(o_ref.dtype)


def _norm_proj(x, modr, layer, mod_base, nw, w, *, rows_per_sample, out_dtype, mode="plain",
               bias=None, gain=None, rope=None, n_norm=0, scale=1.0):
    rows = x.shape[0]
    tm, tn = ROW_TILE, COL_TILE
    n_out = w.shape[1] // 2 if mode == "glu" else w.shape[1]
    head = [
        pl.BlockSpec((tm, D_MODEL), _row_spec),
        pl.BlockSpec((1, D_MODEL), _const_spec),
        _mod_spec(layer, mod_base + 1, rows_per_sample),
        _mod_spec(layer, mod_base + 0, rows_per_sample),
    ]
    args = [x, nw.reshape(1, D_MODEL), modr, modr]
    if mode == "plain":
        body = _proj_body
        specs = head + [pl.BlockSpec((D_MODEL, tn), lambda i, j: (0, j))]
        args += [w]
    elif mode == "glu":
        body = _glu_body
        off = _tiles(n_out, tn)
        specs = head + [
            pl.BlockSpec((D_MODEL, tn), lambda i, j: (0, j)),
            pl.BlockSpec((D_MODEL, tn), lambda i, j: (0, j + off)),
            pl.BlockSpec((1, tn), lambda i, j: (0, j)),
            pl.BlockSpec((1, tn), lambda i, j: (0, j + off)),
        ]
        b2 = bias.reshape(1, -1)
        args += [w, w, b2, b2]
    else:
        body = functools.partial(_headnorm_body, n_norm=n_norm, scale=scale, rope=rope is not None)
        specs = head + [
            pl.BlockSpec((D_MODEL, tn), lambda i, j: (0, j)),
            pl.BlockSpec((1, ATT_HEAD_DIM), _const_spec),
        ]
        args += [w, gain.reshape(1, ATT_HEAD_DIM)]
        if rope is not None:
            seq_tiles = _tiles(SEQ, tm)
            specs += [pl.BlockSpec((tm, ATT_HEAD_DIM), lambda i, j: (i % seq_tiles, 0))] * 2
            args += list(rope)
    return pl.pallas_call(
        body,
        out_shape=jax.ShapeDtypeStruct((rows, n_out), out_dtype),
        grid=(_tiles(rows, tm), _tiles(n_out, tn)),
        in_specs=specs,
        out_specs=pl.BlockSpec((tm, tn), lambda i, j: (i, j)),
        scratch_shapes=[pltpu.VMEM((tm, D_MODEL), BF16)],
        compiler_params=_params("parallel", "arbitrary"),
        name="norm_proj_" + mode,
    )(*args)


def _resid_body(a_ref, w_ref, b_ref, x_ref, g_ref, o_ref):
    acc = _dot(a_ref[...], w_ref[...]) + b_ref[...]
    o_ref[...] = x_ref[...] + g_ref[...] * acc


def _matmul_resid(a, w, bias, x, modr, layer, mod_base, *, rows_per_sample):
    rows, k = a.shape
    tm, tn = ROW_TILE, COL_TILE
    return pl.pallas_call(
        _resid_body,
        out_shape=jax.ShapeDtypeStruct((rows, D_MODEL), F32),
        grid=(_tiles(rows, tm), _tiles(D_MODEL, tn)),
        in_specs=[
            pl.BlockSpec((tm, k), _row_spec),
            pl.BlockSpec((k, tn), lambda i, j: (0, j)),
            pl.BlockSpec((1, tn), lambda i, j: (0, j)),
            pl.BlockSpec((tm, tn), lambda i, j: (i, j)),
            _mod_spec_cols(layer, mod_base + 2, rows_per_sample, tn),
        ],
        out_specs=pl.BlockSpec((tm, tn), lambda i, j: (i, j)),
        compiler_params=_params("parallel", "parallel"),
        name="matmul_resid",
    )(a, w, bias.reshape(1, D_MODEL), x, modr)


def _mod_spec_cols(layer, k, rows_per_sample, tn):
    if rows_per_sample is None:
        index = lambda i, j: (layer, k, BATCH, 0, j)
    else:
        tiles = _tiles(rows_per_sample, ROW_TILE)
        index = lambda i, j: (layer, k, i // tiles, 0, j)
    return pl.BlockSpec((None, None, None, 1, tn), index)


def _ffn_body(x_ref, nw_ref, sc_ref, sh_ref, g_ref, wg_ref, wu_ref, wd_ref, dw_ref, dwb_ref, halo_ref,
              o_ref, h_ref, *, seq_len):
    c = pl.program_id(1)
    tm = x_ref.shape[0]

    @pl.when(c == 0)
    def _():
        h_ref[...] = _modnorm(x_ref[...], nw_ref[...], sc_ref[...], sh_ref[...]).astype(BF16)
        o_ref[...] = jnp.zeros_like(o_ref)

    h = h_ref[...]
    gate = _dot(h, wg_ref[...])
    up = _dot(h, wu_ref[...])
    row = lax.broadcasted_iota(jnp.int32, gate.shape, 0)
    pos = (row + pl.program_id(0) * tm) & (seq_len - 1)
    prev = jnp.where(row == 0, halo_ref[0:1, :], pltpu.roll(gate, 1, 0))
    prev = jnp.where(pos == 0, 0.0, prev)
    nxt = jnp.where(row == tm - 1, halo_ref[1:2, :], pltpu.roll(gate, tm - 1, 0))
    nxt = jnp.where(pos == seq_len - 1, 0.0, nxt)
    conv = prev * dw_ref[0:1, :] + gate * dw_ref[1:2, :] + nxt * dw_ref[2:3, :] + dwb_ref[...]
    a = (_silu(conv) * up).astype(BF16)
    o_ref[...] += _dot(a, wd_ref[...])

    @pl.when(c == pl.num_programs(1) - 1)
    def _():
        o_ref[...] = x_ref[...] + g_ref[...] * o_ref[...]


def _gate_body(x_ref, nw_ref, sc_ref, sh_ref, g_ref, wg_ref, y_ref, wo_ref, o_ref, h_ref):
    c = pl.program_id(1)

    @pl.when(c == 0)
    def _():
        h_ref[...] = _modnorm(x_ref[...], nw_ref[...], sc_ref[...], sh_ref[...]).astype(BF16)
        o_ref[...] = jnp.zeros_like(o_ref)

    a = (_silu(_dot(h_ref[...], wg_ref[...])) * y_ref[...]).astype(BF16)
    o_ref[...] += _dot(a, wo_ref[...])

    @pl.when(c == pl.num_programs(1) - 1)
    def _():
        o_ref[...] = x_ref[...] + g_ref[...] * o_ref[...]


def _halo_body(x_ref, nw_ref, sc_ref, sh_ref, w_ref, o_ref):
    h = _modnorm(x_ref[...], nw_ref[...], sc_ref[...], sh_ref[...]).astype(BF16)
    o_ref[...] = _dot(h, w_ref[...])


def _ffn_halo(x, modr, layer, nw, wg, *, rows_per_sample):
    rows = x.shape[0]
    tm, tn = ROW_TILE, FUSED_COL_TILE
    nt = _tiles(rows, tm)
    t = jnp.arange(nt, dtype=jnp.int32)
    prev = jnp.maximum(t * tm - 1, 0)
    nxt = jnp.minimum((t + 1) * tm, rows - 1)
    idx = jnp.stack([prev, nxt] + [prev] * (SUBLANES - 2), axis=1).reshape(-1)
    sample = jnp.full((nt,), BATCH, jnp.int32) if rows_per_sample is None else t // _tiles(rows_per_sample, tm)
    sample = jnp.repeat(sample, SUBLANES)
    xh = x[idx]
    sc = modr[layer, 4, sample, 0]
    sh = modr[layer, 3, sample, 0]
    m = nt * SUBLANES
    f = wg.shape[1]
    out = pl.pallas_call(
        _halo_body,
        out_shape=jax.ShapeDtypeStruct((m, f), F32),
        grid=(_tiles(f, tn),),
        in_specs=[
            pl.BlockSpec((m, D_MODEL), lambda j: (0, 0)),
            pl.BlockSpec((1, D_MODEL), lambda j: (0, 0)),
            pl.BlockSpec((m, D_MODEL), lambda j: (0, 0)),
            pl.BlockSpec((m, D_MODEL), lambda j: (0, 0)),
            pl.BlockSpec((D_MODEL, tn), lambda j: (0, j)),
        ],
        out_specs=pl.BlockSpec((m, tn), lambda j: (0, j)),
        compiler_params=_params("parallel"),
        name="ffn_halo",
    )(xh, nw.reshape(1, D_MODEL), sc, sh, wg)
    return out.reshape(nt, SUBLANES, f)


def _fused_head(x, modr, layer, mod_base, nw, rows_per_sample):
    specs = [
        pl.BlockSpec((ROW_TILE, D_MODEL), _row_spec),
        pl.BlockSpec((1, D_MODEL), _const_spec),
        _mod_spec(layer, mod_base + 1, rows_per_sample),
        _mod_spec(layer, mod_base + 0, rows_per_sample),
        _mod_spec(layer, mod_base + 2, rows_per_sample),
    ]
    return specs, [x, nw.reshape(1, D_MODEL), modr, modr, modr]


def _conv_ffn(x, modr, layer, nw, wg, wu, wd, dw, dwb, *, rows_per_sample, seq_len):
    rows = x.shape[0]
    tm, tf = ROW_TILE, FUSED_COL_TILE
    f = wg.shape[1]
    halo = _ffn_halo(x, modr, layer, nw, wg, rows_per_sample=rows_per_sample)
    specs, args = _fused_head(x, modr, layer, 3, nw, rows_per_sample)
    specs += [
        pl.BlockSpec((D_MODEL, tf), lambda i, c: (0, c)),
        pl.BlockSpec((D_MODEL, tf), lambda i, c: (0, c)),
        pl.BlockSpec((tf, D_MODEL), lambda i, c: (c, 0)),
        pl.BlockSpec((FFN_CONV_WIDTH, tf), lambda i, c: (0, c)),
        pl.BlockSpec((1, tf), lambda i, c: (0, c)),
        pl.BlockSpec((None, SUBLANES, tf), lambda i, c: (i, 0, c)),
    ]
    args += [wg, wu, wd, dw, dwb.reshape(1, f), halo]
    return pl.pallas_call(
        functools.partial(_ffn_body, seq_len=seq_len),
        out_shape=jax.ShapeDtypeStruct((rows, D_MODEL), F32),
        grid=(_tiles(rows, tm), _tiles(f, tf)),
        in_specs=specs,
        out_specs=pl.BlockSpec((tm, D_MODEL), _row_spec),
        scratch_shapes=[pltpu.VMEM((tm, D_MODEL), BF16)],
        compiler_params=_params("parallel", "arbitrary"),
        name="conv_ffn",
    )(*args)


def _gated_out(x, modr, layer, nw, wg, y, wo, *, rows_per_sample):
    rows = x.shape[0]
    tm, tf = ROW_TILE, FUSED_COL_TILE
    f = wg.shape[1]
    specs, args = _fused_head(x, modr, layer, 0, nw, rows_per_sample)
    specs += [
        pl.BlockSpec((D_MODEL, tf), lambda i, c: (0, c)),
        pl.BlockSpec((tm, tf), lambda i, c: (i, c)),
        pl.BlockSpec((tf, D_MODEL), lambda i, c: (c, 0)),
    ]
    args += [wg, y, wo]
    return pl.pallas_call(
        _gate_body,
        out_shape=jax.ShapeDtypeStruct((rows, D_MODEL), F32),
        grid=(_tiles(rows, tm), _tiles(f, tf)),
        in_specs=specs,
        out_specs=pl.BlockSpec((tm, D_MODEL), _row_spec),
        scratch_shapes=[pltpu.VMEM((tm, D_MODEL), BF16)],
        compiler_params=_params("parallel", "arbitrary"),
        name="gated_out",
    )(*args)


def _ret_body(lg_ref, ql_ref, kl_ref, vl_ref, qc_ref, kc_ref, vc_ref, gn_ref, yl_ref, *rest, with_ctx):
    if with_ctx:
        yc_ref, s_ref = rest
    else:
        yc_ref = None
        (s_ref,) = rest
    head = pl.program_id(1)
    lgf = lg_ref[0, head]
    lgb = lg_ref[1, head]
    c = RET_CHUNK
    cf = float(c)
    n = lax.broadcasted_iota(jnp.int32, (c, c), 0).astype(F32)
    m = lax.broadcasted_iota(jnp.int32, (c, c), 1).astype(F32)
    diff = n - m
    intra = (jnp.where(diff >= 0, jnp.exp(lgf * jnp.maximum(diff, 0.0)), 0.0)
             + jnp.where(diff <= 0, jnp.exp(lgb * jnp.maximum(-diff, 0.0)), 0.0))
    pv = lax.broadcasted_iota(jnp.int32, (c, RET_V_DIM), 0).astype(F32)
    pk = lax.broadcasted_iota(jnp.int32, (c, RET_QK_DIM), 0).astype(F32)
    q_dec_f = jnp.exp(lgf * (pv + 1.0))
    q_dec_b = jnp.exp(lgb * (cf - pv))
    k_dec_f = jnp.exp(lgf * (cf - 1.0 - pk))
    k_dec_b = jnp.exp(lgb * pk)
    chunk_dec_f = jnp.exp(lgf * jnp.full((1, RET_V_DIM), cf, F32))
    chunk_dec_b = jnp.exp(lgb * jnp.full((1, RET_V_DIM), cf, F32))

    def advance(k, v, k_dec, chunk_dec):
        kd = (k.astype(F32) * k_dec).astype(BF16)
        s_ref[...] = s_ref[...] * chunk_dec + _dot_tn(kd, v)

    def fwd(q_ref, k_ref, v_ref, y_ref, r0):
        rows = pl.ds(r0, c)
        q, k, v = q_ref[rows, :], k_ref[rows, :], v_ref[rows, :]
        if y_ref is not None:
            scores = (_dot_nt(q, k) * intra).astype(BF16)
            y_ref[rows, :] = _dot(scores, v) + _dot(q, s_ref[...].astype(BF16)) * q_dec_f
        advance(k, v, k_dec_f, chunk_dec_f)

    def bwd(q_ref, k_ref, v_ref, y_ref, r0):
        rows = pl.ds(r0, c)
        q, k, v = q_ref[rows, :], k_ref[rows, :], v_ref[rows, :]
        if y_ref is not None:
            y = y_ref[rows, :] + _dot(q, s_ref[...].astype(BF16)) * q_dec_b
            mu = jnp.mean(y, axis=-1, keepdims=True)
            yc = y - mu
            var = jnp.mean(yc * yc, axis=-1, keepdims=True)
            y_ref[rows, :] = (yc * lax.rsqrt(var + LN_EPS)) * gn_ref[...]
        advance(k, v, k_dec_b, chunk_dec_b)

    n_ctx = _tiles(CTX_LEN, c)
    n_lat = _tiles(SEQ, c)

    s_ref[...] = jnp.zeros_like(s_ref)
    for j in range(n_ctx):
        fwd(qc_ref, kc_ref, vc_ref, yc_ref, j * c)

    def fwd_lat(j, carry):
        fwd(ql_ref, kl_ref, vl_ref, yl_ref, pl.multiple_of(j * c, c))
        return carry

    lax.fori_loop(0, n_lat, fwd_lat, 0)

    s_ref[...] = jnp.zeros_like(s_ref)
    for j in reversed(range(n_ctx)):
        bwd(qc_ref, kc_ref, vc_ref, yc_ref, j * c)

    def bwd_lat(j, carry):
        bwd(ql_ref, kl_ref, vl_ref, yl_ref, pl.multiple_of((n_lat - 1 - j) * c, c))
        return carry

    lax.fori_loop(0, n_lat, bwd_lat, 0)


def _retention(qkv_l, qkv_c, log_gamma, gn_w, *, with_ctx):
    dk, dv, h = RET_QK_DIM, RET_V_DIM, RET_HEADS
    hv = h * dv
    k_off = h
    v_off = _tiles(2 * h * dk, dv)
    in_specs = [
        pl.BlockSpec((SEQ, dk), lambda b, n, lg: (b, n)),
        pl.BlockSpec((SEQ, dk), lambda b, n, lg: (b, k_off + n)),
        pl.BlockSpec((SEQ, dv), lambda b, n, lg: (b, v_off + n)),
        pl.BlockSpec((CTX_LEN, dk), lambda b, n, lg: (b, n)),
        pl.BlockSpec((CTX_LEN, dk), lambda b, n, lg: (b, k_off + n)),
        pl.BlockSpec((CTX_LEN, dv), lambda b, n, lg: (b, v_off + n)),
        pl.BlockSpec((1, dv), lambda b, n, lg: (0, n)),
    ]
    out_shape = [jax.ShapeDtypeStruct((BATCH * SEQ, hv), F32)]
    out_specs = [pl.BlockSpec((SEQ, dv), lambda b, n, lg: (b, n))]
    if with_ctx:
        out_shape.append(jax.ShapeDtypeStruct((BATCH * CTX_LEN, hv), F32))
        out_specs.append(pl.BlockSpec((CTX_LEN, dv), lambda b, n, lg: (b, n)))
    outs = pl.pallas_call(
        functools.partial(_ret_body, with_ctx=with_ctx),
        out_shape=out_shape,
        grid_spec=pltpu.PrefetchScalarGridSpec(
            num_scalar_prefetch=1,
            grid=(BATCH, h),
            in_specs=in_specs,
            out_specs=out_specs,
            scratch_shapes=[pltpu.VMEM((dk, dv), F32)],
        ),
        compiler_params=_params("parallel", "parallel"),
        name="retention",
    )(log_gamma, qkv_l, qkv_l, qkv_l, qkv_c, qkv_c, qkv_c, gn_w.reshape(1, hv))
    return (outs[0], outs[1]) if with_ctx else (outs[0], None)


def _att_body(q_ref, kc_ref, vc_ref, *rest, with_lat):
    if with_lat:
        kl_ref, vl_ref, o_ref = rest
    else:
        (o_ref,) = rest
    hd = ATT_HEAD_DIM
    for g in range(ATT_GROUPS):
        q = q_ref[:, g * hd:(g + 1) * hd]
        s_c = _dot_nt(q, kc_ref[...])
        mx = jnp.max(s_c, axis=-1, keepdims=True)
        if with_lat:
            s_l = _dot_nt(q, kl_ref[...])
            mx = jnp.maximum(mx, jnp.max(s_l, axis=-1, keepdims=True))
        p_c = jnp.exp(s_c - mx)
        den = jnp.sum(p_c, axis=-1, keepdims=True)
        acc = _dot(p_c.astype(BF16), vc_ref[...])
        if with_lat:
            p_l = jnp.exp(s_l - mx)
            den = den + jnp.sum(p_l, axis=-1, keepdims=True)
            acc = acc + _dot(p_l.astype(BF16), vl_ref[...])
        o_ref[:, g * hd:(g + 1) * hd] = (acc / den).astype(o_ref.dtype)


def _attention(q, kv_c, kv_l):
    hd, kvh = ATT_HEAD_DIM, ATT_KV_HEADS
    gw = ATT_GROUPS * hd
    if kv_l is None:
        grid = (BATCH, kvh)
        q_spec = pl.BlockSpec((CTX_LEN, gw), lambda b, n: (b, n))
        specs = [q_spec,
                 pl.BlockSpec((CTX_LEN, hd), lambda b, n: (b, n)),
                 pl.BlockSpec((CTX_LEN, hd), lambda b, n: (b, kvh + n))]
        args = [q, kv_c, kv_c]
        sem = ("parallel", "parallel")
    else:
        tq = ATT_Q_TILE
        qt = _tiles(SEQ, tq)
        grid = (BATCH, kvh, qt)
        q_spec = pl.BlockSpec((tq, gw), lambda b, n, t: (b * qt + t, n))
        specs = [q_spec,
                 pl.BlockSpec((CTX_LEN, hd), lambda b, n, t: (b, n)),
                 pl.BlockSpec((CTX_LEN, hd), lambda b, n, t: (b, kvh + n)),
                 pl.BlockSpec((SEQ, hd), lambda b, n, t: (b, n)),
                 pl.BlockSpec((SEQ, hd), lambda b, n, t: (b, kvh + n))]
        args = [q, kv_c, kv_c, kv_l, kv_l]
        sem = ("parallel", "parallel", "parallel")
    return pl.pallas_call(
        functools.partial(_att_body, with_lat=kv_l is not None),
        out_shape=jax.ShapeDtypeStruct(q.shape, BF16),
        grid=grid,
        in_specs=specs,
        out_specs=q_spec,
        compiler_params=_params(*sem),
        name="attention",
    )(*args)


def _dwconv_body(ap_ref, a_ref, an_ref, dw_ref, dwb_ref, lnw_ref, lnb_ref, o_ref, buf_ref, cv_ref, *, seq_tiles):
    tm = a_ref.shape[0]
    pos = pl.program_id(0) % seq_tiles
    buf_ref[0:CONV_HALO, :] = jnp.where(pos == 0, 0.0, ap_ref[...])
    buf_ref[CONV_HALO:CONV_HALO + tm, :] = a_ref[...]
    buf_ref[CONV_HALO + tm:, :] = jnp.where(pos == seq_tiles - 1, 0.0, an_ref[...])
    first = CONV_HALO - CONV_WIDTH // 2
    for cc in range(_tiles(D_MODEL, CONV_COL_CHUNK)):
        cols = slice(cc * CONV_COL_CHUNK, (cc + 1) * CONV_COL_CHUNK)
        acc = jnp.broadcast_to(dwb_ref[:, cols], (tm, CONV_COL_CHUNK))
        for d in range(CONV_WIDTH):
            acc = acc + buf_ref[first + d:first + d + tm, cols] * dw_ref[d:d + 1, cols]
        cv_ref[:, cols] = acc
    y = cv_ref[...]
    mu = jnp.mean(y, axis=-1, keepdims=True)
    yc = y - mu
    var = jnp.mean(yc * yc, axis=-1, keepdims=True)
    z = (yc * lax.rsqrt(var + LN_EPS)) * lnw_ref[...] + lnb_ref[...]
    o_ref[...] = _silu(z).astype(o_ref.dtype)


def _dwconv_ln_silu(a, dw, dwb, lnw, lnb, *, seq_len):
    rows = a.shape[0]
    tm = CONV_ROW_TILE
    per = _tiles(tm, CONV_HALO)
    last = _tiles(rows, CONV_HALO) - 1
    vec = lambda: pl.BlockSpec((1, D_MODEL), lambda i: (0, 0))
    return pl.pallas_call(
        functools.partial(_dwconv_body, seq_tiles=_tiles(seq_len, tm)),
        out_shape=jax.ShapeDtypeStruct((rows, D_MODEL), BF16),
        grid=(_tiles(rows, tm),),
        in_specs=[
            pl.BlockSpec((CONV_HALO, D_MODEL), lambda i: (jnp.maximum(i * per - 1, 0), 0)),
            pl.BlockSpec((tm, D_MODEL), lambda i: (i, 0)),
            pl.BlockSpec((CONV_HALO, D_MODEL), lambda i: (jnp.minimum((i + 1) * per, last), 0)),
            pl.BlockSpec((CONV_WIDTH, D_MODEL), lambda i: (0, 0)),
            vec(), vec(), vec(),
        ],
        out_specs=pl.BlockSpec((tm, D_MODEL), lambda i: (i, 0)),
        scratch_shapes=[pltpu.VMEM((tm + 2 * CONV_HALO, D_MODEL), F32), pltpu.VMEM((tm, D_MODEL), F32)],
        compiler_params=_params("parallel"),
        name="dwconv_ln_silu",
    )(a, a, a, dw, dwb.reshape(1, D_MODEL), lnw.reshape(1, D_MODEL), lnb.reshape(1, D_MODEL))


def _rope_tables():
    nf = ATT_HEAD_DIM // 4
    t = jnp.arange(SEQ)
    inv_freq = ROPE_BASE ** (-jnp.arange(nf, dtype=F32) / nf)
    ang_r = (t // GRID_W).astype(F32)[:, None] * inv_freq[None, :]
    ang_c = (t % GRID_W).astype(F32)[:, None] * inv_freq[None, :]
    cos = jnp.concatenate([jnp.cos(ang_r)] * 2 + [jnp.cos(ang_c)] * 2, axis=-1)
    sin = jnp.concatenate([-jnp.sin(ang_r), jnp.sin(ang_r), -jnp.sin(ang_c), jnp.sin(ang_c)], axis=-1)
    return cos, sin


def kernel(x, c, ctx, c_ctx, mod_w, mod_b, norm1_w, norm2_w, ffn_w_gate, ffn_w_up, ffn_dw, ffn_dw_b, ffn_w_down, ret_wq, ret_wk, ret_wv, ret_wg, ret_wo, ret_decay, ret_gn_w, att_wq, att_wkv, att_q_gain, att_k_gain, att_wo, cnv_w1, cnv_b1, cnv_dw, cnv_dw_b, cnv_ln_w, cnv_ln_b, cnv_w2, cnv_b2):
    xl = x.reshape(BATCH * SEQ, D_MODEL)
    xc = ctx.reshape(BATCH * CTX_LEN, D_MODEL)
    streams = lambda: ((SEQ, SEQ), (None, CTX_LEN))

    cin = jnp.concatenate([c, c_ctx[None, :], jnp.zeros((SUBLANES - BATCH - 1, D_MODEL), F32)], axis=0)
    mod = _modulation(cin, mod_w, mod_b)
    modr = mod.reshape(DEPTH, SUBLANES, N_MOD, D_MODEL).transpose(0, 2, 1, 3)[:, :, :, None, :]
    rope = _rope_tables()
    zero_bias = jnp.zeros((D_MODEL,), F32)

    for i in range(DEPTH):
        with_ctx = i < DEPTH - 1
        kind, j = i % N_MIXERS, i // N_MIXERS
        nw1, nw2 = norm1_w[i], norm2_w[i]
        if kind == 0:
            w_qkv = jnp.concatenate([ret_wq[j], ret_wk[j] * (RET_QK_DIM ** -0.5), ret_wv[j]], axis=1).astype(BF16)
            qkv_l = _norm_proj(xl, modr, i, 0, nw1, w_qkv, rows_per_sample=SEQ, out_dtype=BF16)
            qkv_c = _norm_proj(xc, modr, i, 0, nw1, w_qkv, rows_per_sample=None, out_dtype=BF16)
            log_gamma = -jnp.exp(ret_decay[j].astype(F32))
            y_l, y_c = _retention(qkv_l, qkv_c, log_gamma, ret_gn_w[j], with_ctx=with_ctx)
            wg, wo = ret_wg[j].astype(BF16), ret_wo[j].astype(BF16)
            xl = _gated_out(xl, modr, i, nw1, wg, y_l, wo, rows_per_sample=SEQ)
            if with_ctx:
                xc = _gated_out(xc, modr, i, nw1, wg, y_c, wo, rows_per_sample=None)
        elif kind == 1:
            wq, wkv, wo = att_wq[j].astype(BF16), att_wkv[j].astype(BF16), att_wo[j].astype(BF16)
            n_k = ATT_KV_HEADS * ATT_HEAD_DIM
            q_l = _norm_proj(xl, modr, i, 0, nw1, wq, rows_per_sample=SEQ, out_dtype=BF16, mode="headnorm",
                             gain=att_q_gain[j], rope=rope, n_norm=COL_TILE, scale=ATT_HEAD_DIM ** -0.5)
            kv_l = _norm_proj(xl, modr, i, 0, nw1, wkv, rows_per_sample=SEQ, out_dtype=BF16, mode="headnorm",
                              gain=att_k_gain[j], rope=rope, n_norm=n_k)
            kv_c = _norm_proj(xc, modr, i, 0, nw1, wkv, rows_per_sample=None, out_dtype=BF16, mode="headnorm",
                              gain=att_k_gain[j], n_norm=n_k)
            y_l = _attention(q_l, kv_c, kv_l)
            xl = _matmul_resid(y_l, wo, zero_bias, xl, modr, i, 0, rows_per_sample=SEQ)
            if with_ctx:
                q_c = _norm_proj(xc, modr, i, 0, nw1, wq, rows_per_sample=None, out_dtype=BF16, mode="headnorm",
                                 gain=att_q_gain[j], n_norm=COL_TILE, scale=ATT_HEAD_DIM ** -0.5)
                y_c = _attention(q_c, kv_c, None)
                xc = _matmul_resid(y_c, wo, zero_bias, xc, modr, i, 0, rows_per_sample=None)
        else:
            w1, w2 = cnv_w1[j].astype(BF16), cnv_w2[j].astype(BF16)
            outs = []
            for xs, (rps, seq_len) in zip((xl, xc), streams()):
                if xs is xc and not with_ctx:
                    outs.append(xc)
                    continue
                a = _norm_proj(xs, modr, i, 0, nw1, w1, rows_per_sample=rps, out_dtype=F32, mode="glu",
                               bias=cnv_b1[j])
                a = _dwconv_ln_silu(a, cnv_dw[j], cnv_dw_b[j], cnv_ln_w[j], cnv_ln_b[j], seq_len=seq_len)
                outs.append(_matmul_resid(a, w2, cnv_b2[j], xs, modr, i, 0, rows_per_sample=rps))
            xl, xc = outs
        wg, wu, wd = ffn_w_gate[i].astype(BF16), ffn_w_up[i].astype(BF16), ffn_w_down[i].astype(BF16)
        xl = _conv_ffn(xl, modr, i, nw2, wg, wu, wd, ffn_dw[i], ffn_dw_b[i], rows_per_sample=SEQ, seq_len=SEQ)
        if with_ctx:
            xc = _conv_ffn(xc, modr, i, nw2, wg, wu, wd, ffn_dw[i], ffn_dw_b[i], rows_per_sample=None,
                           seq_len=CTX_LEN)
    return xl.reshape(BATCH, SEQ, D_MODEL)
```

```python
import functools

import jax
import jax.numpy as jnp
from jax import lax
from jax.experimental import pallas as pl
from jax.experimental.pallas import tpu as pltpu

D_MODEL = 2048
BATCH = 4
SEQ = 4096
DEPTH = 4
GRID_W = 64
CTX_LEN = 256
N_MIXERS = 3
RET_HEADS = 8
RET_QK_DIM = D_MODEL // RET_HEADS
RET_V_DIM = 2 * RET_QK_DIM
ATT_HEAD_DIM = 128
ATT_HEADS = D_MODEL // ATT_HEAD_DIM
ATT_KV_HEADS = 4
ATT_GROUPS = ATT_HEADS // ATT_KV_HEADS
ROPE_BASE = 10000.0
CONV_WIDTH = 31
FFN_DIM = ((8 * D_MODEL // 3 + 255) // 256) * 256
FFN_CONV_WIDTH = 3
N_MOD = 6
RMS_EPS = 1e-6
LN_EPS = 1e-5

ATT_Q_SCALE = ATT_HEAD_DIM ** -0.5 * 1.4426950408889634

BF16 = jnp.bfloat16
F32 = jnp.float32

LANES = 128
SUBLANES = 8
VMEM_LIMIT_BYTES = 56 * 1024 * 1024

PROJ_ROW_TILE = 1024
FUSED_ROW_TILE = 512
COL_TILE = 1024
FUSED_COL_TILE = 512
ATT_Q_TILE = 512
ATT_KV_CHUNK = 512
RET_SCAN_CHUNK = 256
CONV_ROW_TILE = 128
CONV_HALO = 16
CONV_COL_CHUNK = 256


def _tiles(extent, tile):
    assert extent % tile == 0, (extent, tile)
    return extent // tile


def _dot(a, b):
    return jnp.dot(a, b, preferred_element_type=F32)


def _dot_nt(a, b):
    return lax.dot_general(a, b, (((1,), (1,)), ((), ())), preferred_element_type=F32)


def _dot_tn(a, b):
    return lax.dot_general(a, b, (((0,), (0,)), ((), ())), preferred_element_type=F32)


def _sigmoid(x):
    return 1.0 / (1.0 + jnp.exp(-x))


def _silu(x):
    return x * _sigmoid(x)


def _modnorm(x, nw, sc, sh):
    ms = jnp.mean(x * x, axis=-1, keepdims=True)
    y = x * lax.rsqrt(ms + RMS_EPS)
    return y * (nw * (1.0 + sc)) + sh


def _params(*semantics):
    return pltpu.CompilerParams(dimension_semantics=semantics, vmem_limit_bytes=VMEM_LIMIT_BYTES)


def _mod_spec(layer, k, rows_per_sample, tm):
    if rows_per_sample is None:
        index = lambda i, j: (layer, k, BATCH, 0, 0)
    else:
        tiles = _tiles(rows_per_sample, tm)
        index = lambda i, j: (layer, k, i // tiles, 0, 0)
    return pl.BlockSpec((None, None, None, 1, D_MODEL), index)


def _row_spec(i, j):
    return (i, 0)


def _const_spec(i, j):
    return (0, 0)


def _mod_body(c_ref, w_ref, b_ref, o_ref):
    c = c_ref[...]
    o_ref[...] = _dot(_silu(c).astype(BF16), w_ref[...].astype(BF16)) + b_ref[...]


def _modulation(cin, mod_w, mod_b):
    n = N_MOD * D_MODEL
    tn = COL_TILE
    return pl.pallas_call(
        _mod_body,
        out_shape=jax.ShapeDtypeStruct((DEPTH, SUBLANES, n), F32),
        grid=(DEPTH, _tiles(n, tn)),
        in_specs=[
            pl.BlockSpec((SUBLANES, D_MODEL), lambda l, j: (0, 0)),
            pl.BlockSpec((None, D_MODEL, tn), lambda l, j: (l, 0, j)),
            pl.BlockSpec((None, 1, tn), lambda l, j: (l, 0, j)),
        ],
        out_specs=pl.BlockSpec((None, SUBLANES, tn), lambda l, j: (l, 0, j)),
        compiler_params=_params("parallel", "parallel"),
        name="modulation",
    )(cin, mod_w, mod_b.reshape(DEPTH, 1, n))


def _fill_h(x_ref, nw_ref, sc_ref, sh_ref, h_ref):
    @pl.when(pl.program_id(1) == 0)
    def _():
        h_ref[...] = _modnorm(x_ref[...], nw_ref[...], sc_ref[...], sh_ref[...]).astype(BF16)


def _proj_body(x_ref, nw_ref, sc_ref, sh_ref, w_ref, o_ref, h_ref):
    _fill_h(x_ref, nw_ref, sc_ref, sh_ref, h_ref)
    o_ref[...] = _dot(h_ref[...], w_ref[...]).astype(o_ref.dtype)


def _glu_body(x_ref, nw_ref, sc_ref, sh_ref, wa_ref, wb_ref, ba_ref, bb_ref, o_ref, h_ref):
    _fill_h(x_ref, nw_ref, sc_ref, sh_ref, h_ref)
    h = h_ref[...]
    a = _dot(h, wa_ref[...]) + ba_ref[...]
    b = _dot(h, wb_ref[...]) + bb_ref[...]
    o_ref[...] = (a * _sigmoid(b)).astype(o_ref.dtype)


def _headnorm_body(x_ref, nw_ref, sc_ref, sh_ref, w_ref, gain_ref, *rest, n_norm, scale, rope):
    if rope:
        cos_ref, sin_ref, o_ref, h_ref = rest
    else:
        o_ref, h_ref = rest
    _fill_h(x_ref, nw_ref, sc_ref, sh_ref, h_ref)
    acc = _dot(h_ref[...], w_ref[...])
    hd = ATT_HEAD_DIM
    for head in range(_tiles(n_norm, hd)):
        a = acc[:, head * hd:(head + 1) * hd]
        ms = jnp.mean(a * a, axis=-1, keepdims=True)
        a = (a * lax.rsqrt(ms + RMS_EPS)) * gain_ref[...]
        if rope:
            lane = lax.broadcasted_iota(jnp.int32, a.shape, 1)
            quarter = hd // 4
            partner = jnp.where((lane & (2 * quarter - 1)) < quarter,
                                pltpu.roll(a, hd - quarter, 1), pltpu.roll(a, quarter, 1))
            a = a * cos_ref[...] + partner * sin_ref[...]
        if scale != 1.0:
            a = a * scale
        o_ref[:, head * hd:(head + 1) * hd] = a.astype(o_ref.dtype)
    if n_norm < acc.shape[1]:
        o_ref[:, n_norm:] = acc[:, n_norm:].astype(o_ref.dtype)


def _norm_proj(x, modr, layer, mod_base, nw, w, *, rows_per_sample, out_dtype, mode="plain",
               bias=None, gain=None, rope=None, n_norm=0, scale=1.0):
    rows = x.shape[0]
    tm, tn = PROJ_ROW_TILE, COL_TILE
    n_out = w.shape[1] // 2 if mode == "glu" else w.shape[1]
    head = [
        pl.BlockSpec((tm, D_MODEL), _row_spec),
        pl.BlockSpec((1, D_MODEL), _const_spec),
        _mod_spec(layer, mod_base + 1, rows_per_sample, tm),
        _mod_spec(layer, mod_base + 0, rows_per_sample, tm),
    ]
    args = [x, nw.reshape(1, D_MODEL), modr, modr]
    if mode == "plain":
        body = _proj_body
        specs = head + [pl.BlockSpec((D_MODEL, tn), lambda i, j: (0, j))]
        args += [w]
    elif mode == "glu":
        body = _glu_body
        off = _tiles(n_out, tn)
        specs = head + [
            pl.BlockSpec((D_MODEL, tn), lambda i, j: (0, j)),
            pl.BlockSpec((D_MODEL, tn), lambda i, j: (0, j + off)),
            pl.BlockSpec((1, tn), lambda i, j: (0, j)),
            pl.BlockSpec((1, tn), lambda i, j: (0, j + off)),
        ]
        b2 = bias.reshape(1, -1)
        args += [w, w, b2, b2]
    else:
        body = functools.partial(_headnorm_body, n_norm=n_norm, scale=scale, rope=rope is not None)
        specs = head + [
            pl.BlockSpec((D_MODEL, tn), lambda i, j: (0, j)),
            pl.BlockSpec((1, ATT_HEAD_DIM), _const_spec),
        ]
        args += [w, gain.reshape(1, ATT_HEAD_DIM)]
        if rope is not None:
            seq_tiles = _tiles(SEQ, tm)
            specs += [pl.BlockSpec((tm, ATT_HEAD_DIM), lambda i, j: (i % seq_tiles, 0))] * 2
            args += list(rope)
    return pl.pallas_call(
        body,
        out_shape=jax.ShapeDtypeStruct((rows, n_out), out_dtype),
        grid=(_tiles(rows, tm), _tiles(n_out, tn)),
        in_specs=specs,
        out_specs=pl.BlockSpec((tm, tn), lambda i, j: (i, j)),
        scratch_shapes=[pltpu.VMEM((tm, D_MODEL), BF16)],
        compiler_params=_params("parallel", "arbitrary"),
        name="norm_proj_" + mode,
    )(*args)


def _resid_body(a_ref, w_ref, b_ref, x_ref, g_ref, o_ref):
    acc = _dot(a_ref[...], w_ref[...]) + b_ref[...]
    o_ref[...] = x_ref[...] + g_ref[...] * acc


def _matmul_resid(a, w, bias, x, modr, layer, mod_base, *, rows_per_sample):
    rows, k = a.shape
    tm, tn = PROJ_ROW_TILE, COL_TILE
    return pl.pallas_call(
        _resid_body,
        out_shape=jax.ShapeDtypeStruct((rows, D_MODEL), F32),
        grid=(_tiles(rows, tm), _tiles(D_MODEL, tn)),
        in_specs=[
            pl.BlockSpec((tm, k), _row_spec),
            pl.BlockSpec((k, tn), lambda i, j: (0, j)),
            pl.BlockSpec((1, tn), lambda i, j: (0, j)),
            pl.BlockSpec((tm, tn), lambda i, j: (i, j)),
            _mod_spec_cols(layer, mod_base + 2, rows_per_sample, tm, tn),
        ],
        out_specs=pl.BlockSpec((tm, tn), lambda i, j: (i, j)),
        compiler_params=_params("parallel", "parallel"),
        name="matmul_resid",
    )(a, w, bias.reshape(1, D_MODEL), x, modr)


def _mod_spec_cols(layer, k, rows_per_sample, tm, tn):
    if rows_per_sample is None:
        index = lambda i, j: (layer, k, BATCH, 0, j)
    else:
        tiles = _tiles(rows_per_sample, tm)
        index = lambda i, j: (layer, k, i // tiles, 0, j)
    return pl.BlockSpec((None, None, None, 1, tn), index)


def _ffn_body(x_ref, nw_ref, sc_ref, sh_ref, g_ref, wg_ref, wu_ref, wd_ref, dw_ref, dwb_ref, halo_ref,
              o_ref, h_ref, *, seq_len):
    c = pl.program_id(1)
    tm = x_ref.shape[0]

    @pl.when(c == 0)
    def _():
        h_ref[...] = _modnorm(x_ref[...], nw_ref[...], sc_ref[...], sh_ref[...]).astype(BF16)
        o_ref[...] = jnp.zeros_like(o_ref)

    h = h_ref[...]
    gate = _dot(h, wg_ref[...])
    up = _dot(h, wu_ref[...])
    row = lax.broadcasted_iota(jnp.int32, gate.shape, 0)
    pos = (row + pl.program_id(0) * tm) & (seq_len - 1)
    prev = jnp.where(row == 0, halo_ref[0:1, :], pltpu.roll(gate, 1, 0))
    prev = jnp.where(pos == 0, 0.0, prev)
    nxt = jnp.where(row == tm - 1, halo_ref[1:2, :], pltpu.roll(gate, tm - 1, 0))
    nxt = jnp.where(pos == seq_len - 1, 0.0, nxt)
    conv = prev * dw_ref[0:1, :] + gate * dw_ref[1:2, :] + nxt * dw_ref[2:3, :] + dwb_ref[...]
    a = (_silu(conv) * up).astype(BF16)
    o_ref[...] += _dot(a, wd_ref[...])

    @pl.when(c == pl.num_programs(1) - 1)
    def _():
        o_ref[...] = x_ref[...] + g_ref[...] * o_ref[...]


def _gate_body(x_ref, nw_ref, sc_ref, sh_ref, g_ref, wg_ref, y_ref, wo_ref, o_ref, h_ref):
    c = pl.program_id(1)

    @pl.when(c == 0)
    def _():
        h_ref[...] = _modnorm(x_ref[...], nw_ref[...], sc_ref[...], sh_ref[...]).astype(BF16)
        o_ref[...] = jnp.zeros_like(o_ref)

    a = (_silu(_dot(h_ref[...], wg_ref[...])) * y_ref[...]).astype(BF16)
    o_ref[...] += _dot(a, wo_ref[...])

    @pl.when(c == pl.num_programs(1) - 1)
    def _():
        o_ref[...] = x_ref[...] + g_ref[...] * o_ref[...]


def _halo_body(x_ref, nw_ref, sc_ref, sh_ref, w_ref, o_ref):
    h = _modnorm(x_ref[...], nw_ref[...], sc_ref[...], sh_ref[...]).astype(BF16)
    o_ref[...] = _dot(h, w_ref[...])


def _ffn_halo(x, modr, layer, nw, wg, *, rows_per_sample):
    rows = x.shape[0]
    tm, tn = FUSED_ROW_TILE, FUSED_COL_TILE
    nt = _tiles(rows, tm)
    t = jnp.arange(nt, dtype=jnp.int32)
    prev = jnp.maximum(t * tm - 1, 0)
    nxt = jnp.minimum((t + 1) * tm, rows - 1)
    idx = jnp.stack([prev, nxt] + [prev] * (SUBLANES - 2), axis=1).reshape(-1)
    sample = jnp.full((nt,), BATCH, jnp.int32) if rows_per_sample is None else t // _tiles(rows_per_sample, tm)
    sample = jnp.repeat(sample, SUBLANES)
    xh = x[idx]
    sc = modr[layer, 4, sample, 0]
    sh = modr[layer, 3, sample, 0]
    m = nt * SUBLANES
    f = wg.shape[1]
    out = pl.pallas_call(
        _halo_body,
        out_shape=jax.ShapeDtypeStruct((m, f), F32),
        grid=(_tiles(f, tn),),
        in_specs=[
            pl.BlockSpec((m, D_MODEL), lambda j: (0, 0)),
            pl.BlockSpec((1, D_MODEL), lambda j: (0, 0)),
            pl.BlockSpec((m, D_MODEL), lambda j: (0, 0)),
            pl.BlockSpec((m, D_MODEL), lambda j: (0, 0)),
            pl.BlockSpec((D_MODEL, tn), lambda j: (0, j)),
        ],
        out_specs=pl.BlockSpec((m, tn), lambda j: (0, j)),
        compiler_params=_params("parallel"),
        name="ffn_halo",
    )(xh, nw.reshape(1, D_MODEL), sc, sh, wg)
    return out.reshape(nt, SUBLANES, f)


def _fused_head(x, modr, layer, mod_base, nw, rows_per_sample):
    specs = [
        pl.BlockSpec((FUSED_ROW_TILE, D_MODEL), _row_spec),
        pl.BlockSpec((1, D_MODEL), _const_spec),
        _mod_spec(layer, mod_base + 1, rows_per_sample, FUSED_ROW_TILE),
        _mod_spec(layer, mod_base + 0, rows_per_sample, FUSED_ROW_TILE),
        _mod_spec(layer, mod_base + 2, rows_per_sample, FUSED_ROW_TILE),
    ]
    return specs, [x, nw.reshape(1, D_MODEL), modr, modr, modr]


def _conv_ffn(x, modr, layer, nw, wg, wu, wd, dw, dwb, *, rows_per_sample, seq_len):
    rows = x.shape[0]
    tm, tf = FUSED_ROW_TILE, FUSED_COL_TILE
    f = wg.shape[1]
    halo = _ffn_halo(x, modr, layer, nw, wg, rows_per_sample=rows_per_sample)
    specs, args = _fused_head(x, modr, layer, 3, nw, rows_per_sample)
    specs += [
        pl.BlockSpec((D_MODEL, tf), lambda i, c: (0, c)),
        pl.BlockSpec((D_MODEL, tf), lambda i, c: (0, c)),
        pl.BlockSpec((tf, D_MODEL), lambda i, c: (c, 0)),
        pl.BlockSpec((FFN_CONV_WIDTH, tf), lambda i, c: (0, c)),
        pl.BlockSpec((1, tf), lambda i, c: (0, c)),
        pl.BlockSpec((None, SUBLANES, tf), lambda i, c: (i, 0, c)),
    ]
    args += [wg, wu, wd, dw, dwb.reshape(1, f), halo]
    return pl.pallas_call(
        functools.partial(_ffn_body, seq_len=seq_len),
        out_shape=jax.ShapeDtypeStruct((rows, D_MODEL), F32),
        grid=(_tiles(rows, tm), _tiles(f, tf)),
        in_specs=specs,
        out_specs=pl.BlockSpec((tm, D_MODEL), _row_spec),
        scratch_shapes=[pltpu.VMEM((tm, D_MODEL), BF16)],
        compiler_params=_params("parallel", "arbitrary"),
        name="conv_ffn",
    )(*args)


def _gated_out(x, modr, layer, nw, wg, y, wo, *, rows_per_sample):
    rows = x.shape[0]
    tm, tf = FUSED_ROW_TILE, FUSED_COL_TILE
    f = wg.shape[1]
    specs, args = _fused_head(x, modr, layer, 0, nw, rows_per_sample)
    specs += [
        pl.BlockSpec((D_MODEL, tf), lambda i, c: (0, c)),
        pl.BlockSpec((tm, tf), lambda i, c: (i, c)),
        pl.BlockSpec((tf, D_MODEL), lambda i, c: (c, 0)),
    ]
    args += [wg, y, wo]
    return pl.pallas_call(
        _gate_body,
        out_shape=jax.ShapeDtypeStruct((rows, D_MODEL), F32),
        grid=(_tiles(rows, tm), _tiles(f, tf)),
        in_specs=specs,
        out_specs=pl.BlockSpec((tm, D_MODEL), _row_spec),
        scratch_shapes=[pltpu.VMEM((tm, D_MODEL), BF16)],
        compiler_params=_params("parallel", "arbitrary"),
        name="gated_out",
    )(*args)


def _ret_body(lg_ref, ql_ref, kl_ref, vl_ref, qc_ref, kc_ref, vc_ref, gn_ref, yl_ref, *rest, with_ctx):
    if with_ctx:
        yc_ref, sf_ref, sb_ref = rest
    else:
        sf_ref, sb_ref = rest
    head = pl.program_id(1)
    lgf = lg_ref[0, head]
    lgb = lg_ref[1, head]
    c = RET_SCAN_CHUNK
    cf = float(c)
    n = lax.broadcasted_iota(jnp.int32, (c, c), 0).astype(F32)
    m = lax.broadcasted_iota(jnp.int32, (c, c), 1).astype(F32)
    diff = n - m
    intra = (jnp.where(diff >= 0, jnp.exp(lgf * jnp.maximum(diff, 0.0)), 0.0)
             + jnp.where(diff <= 0, jnp.exp(lgb * jnp.maximum(-diff, 0.0)), 0.0))
    pv = lax.broadcasted_iota(jnp.int32, (c, RET_V_DIM), 0).astype(F32)
    pk = lax.broadcasted_iota(jnp.int32, (c, RET_QK_DIM), 0).astype(F32)
    q_dec_f = jnp.exp(lgf * (pv + 1.0))
    q_dec_b = jnp.exp(lgb * (cf - pv))
    k_dec_f = jnp.exp(lgf * (cf - 1.0 - pk))
    k_dec_b = jnp.exp(lgb * pk)
    chunk_dec_f = jnp.exp(lgf * jnp.full((1, RET_V_DIM), cf, F32))
    chunk_dec_b = jnp.exp(lgb * jnp.full((1, RET_V_DIM), cf, F32))

    def group_norm(y):
        mu = jnp.mean(y, axis=-1, keepdims=True)
        yc = y - mu
        var = jnp.mean(yc * yc, axis=-1, keepdims=True)
        return (yc * lax.rsqrt(var + LN_EPS)) * gn_ref[...]

    def intra_term(q, k, v):
        return _dot((_dot_nt(q, k) * intra).astype(BF16), v)

    def key_state(k, v, k_dec):
        return _dot_tn((k.astype(F32) * k_dec).astype(BF16), v)

    assert CTX_LEN == c
    qc, kc, vc = qc_ref[...], kc_ref[...], vc_ref[...]
    if with_ctx:
        yc_ref[...] = group_norm(intra_term(qc, kc, vc))
    sf_ref[...] = key_state(kc, vc, k_dec_f)
    sb_ref[...] = key_state(kc, vc, k_dec_b)

    n_lat = _tiles(SEQ, c)
    half = _tiles(n_lat, 2)

    def visit(j, second):
        rows_f = pl.ds(pl.multiple_of(j * c, c), c)
        rows_b = pl.ds(pl.multiple_of((n_lat - 1 - j) * c, c), c)
        q, k, v = ql_ref[rows_f, :], kl_ref[rows_f, :], vl_ref[rows_f, :]
        y = intra_term(q, k, v) + _dot(q, sf_ref[...].astype(BF16)) * q_dec_f
        yl_ref[rows_f, :] = group_norm(y + yl_ref[rows_f, :]) if second else y
        sf_ref[...] = sf_ref[...] * chunk_dec_f + key_state(k, v, k_dec_f)
        q, k, v = ql_ref[rows_b, :], kl_ref[rows_b, :], vl_ref[rows_b, :]
        y = _dot(q, sb_ref[...].astype(BF16)) * q_dec_b
        yl_ref[rows_b, :] = group_norm(y + yl_ref[rows_b, :]) if second else y
        sb_ref[...] = sb_ref[...] * chunk_dec_b + key_state(k, v, k_dec_b)

    def first_half(j, carry):
        visit(j, False)
        return carry

    def second_half(j, carry):
        visit(j, True)
        return carry

    lax.fori_loop(0, half, first_half, 0)
    lax.fori_loop(half, n_lat, second_half, 0)


def _retention(qkv_l, qkv_c, log_gamma, gn_w, *, with_ctx):
    dk, dv, h = RET_QK_DIM, RET_V_DIM, RET_HEADS
    hv = h * dv
    k_off = h
    v_off = _tiles(2 * h * dk, dv)
    in_specs = [
        pl.BlockSpec((SEQ, dk), lambda b, n, lg: (b, n)),
        pl.BlockSpec((SEQ, dk), lambda b, n, lg: (b, k_off + n)),
        pl.BlockSpec((SEQ, dv), lambda b, n, lg: (b, v_off + n)),
        pl.BlockSpec((CTX_LEN, dk), lambda b, n, lg: (b, n)),
        pl.BlockSpec((CTX_LEN, dk), lambda b, n, lg: (b, k_off + n)),
        pl.BlockSpec((CTX_LEN, dv), lambda b, n, lg: (b, v_off + n)),
        pl.BlockSpec((1, dv), lambda b, n, lg: (0, n)),
    ]
    out_shape = [jax.ShapeDtypeStruct((BATCH * SEQ, hv), F32)]
    out_specs = [pl.BlockSpec((SEQ, dv), lambda b, n, lg: (b, n))]
    if with_ctx:
        out_shape.append(jax.ShapeDtypeStruct((BATCH * CTX_LEN, hv), F32))
        out_specs.append(pl.BlockSpec((CTX_LEN, dv), lambda b, n, lg: (b, n)))
    outs = pl.pallas_call(
        functools.partial(_ret_body, with_ctx=with_ctx),
        out_shape=out_shape,
        grid_spec=pltpu.PrefetchScalarGridSpec(
            num_scalar_prefetch=1,
            grid=(BATCH, h),
            in_specs=in_specs,
            out_specs=out_specs,
            scratch_shapes=[pltpu.VMEM((dk, dv), F32), pltpu.VMEM((dk, dv), F32)],
        ),
        compiler_params=_params("parallel", "parallel"),
        name="retention",
    )(log_gamma, qkv_l, qkv_l, qkv_l, qkv_c, qkv_c, qkv_c, gn_w.reshape(1, hv))
    return (outs[0], outs[1]) if with_ctx else (outs[0], None)


def _att_body(q_ref, kc_ref, vc_ref, *rest, with_lat):
    if with_lat:
        kl_ref, vl_ref, o_ref, vtc_ref, vtl_ref = rest
    else:
        o_ref, vtc_ref = rest
    hd = ATT_HEAD_DIM
    tq = q_ref.shape[0]
    kvc = ATT_KV_CHUNK

    @pl.when(pl.program_id(2) == 0)
    def _():
        vtc_ref[...] = vc_ref[...].astype(F32).T.astype(BF16)
        if with_lat:
            for c in range(_tiles(SEQ, kvc)):
                vtl_ref[:, c * kvc:(c + 1) * kvc] = vl_ref[c * kvc:(c + 1) * kvc, :].astype(F32).T.astype(BF16)

    q_all = jnp.concatenate([q_ref[:, g * hd:(g + 1) * hd] for g in range(ATT_GROUPS)], axis=0)
    chunks = [(kc_ref, vtc_ref, 0, CTX_LEN)]
    if with_lat:
        chunks += [(kl_ref, vtl_ref, c * kvc, kvc) for c in range(_tiles(SEQ, kvc))]
    def scores(chunk):
        k_ref, _, r0, n = chunk
        return _dot_nt(k_ref[r0:r0 + n, :], q_all)

    def weighted_values(chunk, p):
        _, vt_ref, r0, n = chunk
        return _dot(vt_ref[:, r0:r0 + n], p)

    mx = den = acc = None
    pending = None
    s_next = scores(chunks[0])
    for idx, chunk in enumerate(chunks):
        s = s_next
        if idx + 1 < len(chunks):
            s_next = scores(chunks[idx + 1])
        if pending is not None:
            pv = weighted_values(pending[0], pending[1])
            acc = pv if acc is None else acc * pending[2] + pv
        mx_new = jnp.max(s, axis=0, keepdims=True)
        alpha = None
        if mx is not None:
            mx_new = jnp.maximum(mx, mx_new)
            alpha = jnp.exp2(mx - mx_new)
        p = jnp.exp2(s - mx_new)
        p_sum = jnp.sum(p, axis=0, keepdims=True)
        den = p_sum if den is None else den * alpha + p_sum
        mx = mx_new
        pending = (chunk, p.astype(BF16), alpha)
    pv = weighted_values(pending[0], pending[1])
    acc = pv if acc is None else acc * pending[2] + pv
    out = acc / den
    for g in range(ATT_GROUPS):
        o_ref[:, g * hd:(g + 1) * hd] = out[:, g * tq:(g + 1) * tq].T.astype(o_ref.dtype)


def _attention(q, kv_c, kv_l):
    hd, kvh = ATT_HEAD_DIM, ATT_KV_HEADS
    gw = ATT_GROUPS * hd
    n_q = CTX_LEN if kv_l is None else SEQ
    tq = min(ATT_Q_TILE, n_q)
    qt = _tiles(n_q, tq)
    q_spec = pl.BlockSpec((tq, gw), lambda b, n, t: (b * qt + t, n))
    specs = [q_spec,
             pl.BlockSpec((CTX_LEN, hd), lambda b, n, t: (b, n)),
             pl.BlockSpec((CTX_LEN, hd), lambda b, n, t: (b, kvh + n))]
    args = [q, kv_c, kv_c]
    scratch = [pltpu.VMEM((hd, CTX_LEN), BF16)]
    if kv_l is not None:
        specs += [pl.BlockSpec((SEQ, hd), lambda b, n, t: (b, n)),
                  pl.BlockSpec((SEQ, hd), lambda b, n, t: (b, kvh + n))]
        args += [kv_l, kv_l]
        scratch += [pltpu.VMEM((hd, SEQ), BF16)]
    return pl.pallas_call(
        functools.partial(_att_body, with_lat=kv_l is not None),
        out_shape=jax.ShapeDtypeStruct(q.shape, BF16),
        grid=(BATCH, kvh, qt),
        in_specs=specs,
        out_specs=q_spec,
        scratch_shapes=scratch,
        compiler_params=_params("parallel", "parallel", "arbitrary"),
        name="attention",
    )(*args)


def _dwconv_body(ap_ref, a_ref, an_ref, dw_ref, dwb_ref, lnw_ref, lnb_ref, o_ref, buf_ref, cv_ref, *, seq_tiles):
    tm = a_ref.shape[0]
    pos = pl.program_id(0) % seq_tiles
    buf_ref[0:CONV_HALO, :] = jnp.where(pos == 0, 0.0, ap_ref[...])
    buf_ref[CONV_HALO:CONV_HALO + tm, :] = a_ref[...]
    buf_ref[CONV_HALO + tm:, :] = jnp.where(pos == seq_tiles - 1, 0.0, an_ref[...])
    first = CONV_HALO - CONV_WIDTH // 2
    rows = tm + 2 * CONV_HALO
    for cc in range(_tiles(D_MODEL, CONV_COL_CHUNK)):
        cols = slice(cc * CONV_COL_CHUNK, (cc + 1) * CONV_COL_CHUNK)
        acc = jnp.broadcast_to(dwb_ref[:, cols], (tm, CONV_COL_CHUNK))
        window = buf_ref[:, cols]
        for shift in range(SUBLANES):
            taps = [d for d in range(CONV_WIDTH) if (first + d) % SUBLANES == shift]
            shifted = window if shift == 0 else pltpu.roll(window, rows - shift, 0)
            for d in taps:
                r0 = first + d - shift
                acc = acc + shifted[r0:r0 + tm, :] * dw_ref[d:d + 1, cols]
        cv_ref[:, cols] = acc
    y = cv_ref[...]
    mu = jnp.mean(y, axis=-1, keepdims=True)
    yc = y - mu
    var = jnp.mean(yc * yc, axis=-1, keepdims=True)
    z = (yc * lax.rsqrt(var + LN_EPS)) * lnw_ref[...] + lnb_ref[...]
    o_ref[...] = _silu(z).astype(o_ref.dtype)


def _dwconv_ln_silu(a, dw, dwb, lnw, lnb, *, seq_len):
    rows = a.shape[0]
    tm = CONV_ROW_TILE
    per = _tiles(tm, CONV_HALO)
    last = _tiles(rows, CONV_HALO) - 1
    vec = lambda: pl.BlockSpec((1, D_MODEL), lambda i: (0, 0))
    return pl.pallas_call(
        functools.partial(_dwconv_body, seq_tiles=_tiles(seq_len, tm)),
        out_shape=jax.ShapeDtypeStruct((rows, D_MODEL), BF16),
        grid=(_tiles(rows, tm),),
        in_specs=[
            pl.BlockSpec((CONV_HALO, D_MODEL), lambda i: (jnp.maximum(i * per - 1, 0), 0)),
            pl.BlockSpec((tm, D_MODEL), lambda i: (i, 0)),
            pl.BlockSpec((CONV_HALO, D_MODEL), lambda i: (jnp.minimum((i + 1) * per, last), 0)),
            pl.BlockSpec((CONV_WIDTH, D_MODEL), lambda i: (0, 0)),
            vec(), vec(), vec(),
        ],
        out_specs=pl.BlockSpec((tm, D_MODEL), lambda i: (i, 0)),
        scratch_shapes=[pltpu.VMEM((tm + 2 * CONV_HALO, D_MODEL), F32), pltpu.VMEM((tm, D_MODEL), F32)],
        compiler_params=_params("parallel"),
        name="dwconv_ln_silu",
    )(a, a, a, dw, dwb.reshape(1, D_MODEL), lnw.reshape(1, D_MODEL), lnb.reshape(1, D_MODEL))


def _rope_tables():
    nf = ATT_HEAD_DIM // 4
    t = jnp.arange(SEQ)
    inv_freq = ROPE_BASE ** (-jnp.arange(nf, dtype=F32) / nf)
    ang_r = (t // GRID_W).astype(F32)[:, None] * inv_freq[None, :]
    ang_c = (t % GRID_W).astype(F32)[:, None] * inv_freq[None, :]
    cos = jnp.concatenate([jnp.cos(ang_r)] * 2 + [jnp.cos(ang_c)] * 2, axis=-1)
    sin = jnp.concatenate([-jnp.sin(ang_r), jnp.sin(ang_r), -jnp.sin(ang_c), jnp.sin(ang_c)], axis=-1)
    return cos, sin


def kernel(x, c, ctx, c_ctx, mod_w, mod_b, norm1_w, norm2_w, ffn_w_gate, ffn_w_up, ffn_dw, ffn_dw_b, ffn_w_down, ret_wq, ret_wk, ret_wv, ret_wg, ret_wo, ret_decay, ret_gn_w, att_wq, att_wkv, att_q_gain, att_k_gain, att_wo, cnv_w1, cnv_b1, cnv_dw, cnv_dw_b, cnv_ln_w, cnv_ln_b, cnv_w2, cnv_b2):
    xl = x.reshape(BATCH * SEQ, D_MODEL)
    xc = ctx.reshape(BATCH * CTX_LEN, D_MODEL)
    streams = lambda: ((SEQ, SEQ), (None, CTX_LEN))

    cin = jnp.concatenate([c, c_ctx[None, :], jnp.zeros((SUBLANES - BATCH - 1, D_MODEL), F32)], axis=0)
    mod = _modulation(cin, mod_w, mod_b)
    modr = mod.reshape(DEPTH, SUBLANES, N_MOD, D_MODEL).transpose(0, 2, 1, 3)[:, :, :, None, :]
    rope = _rope_tables()
    zero_bias = jnp.zeros((D_MODEL,), F32)

    for i in range(DEPTH):
        with_ctx = i < DEPTH - 1
        kind, j = i % N_MIXERS, i // N_MIXERS
        nw1, nw2 = norm1_w[i], norm2_w[i]
        if kind == 0:
            w_qkv = jnp.concatenate([ret_wq[j], ret_wk[j] * (RET_QK_DIM ** -0.5), ret_wv[j]], axis=1).astype(BF16)
            qkv_l = _norm_proj(xl, modr, i, 0, nw1, w_qkv, rows_per_sample=SEQ, out_dtype=BF16)
            qkv_c = _norm_proj(xc, modr, i, 0, nw1, w_qkv, rows_per_sample=None, out_dtype=BF16)
            log_gamma = -jnp.exp(ret_decay[j].astype(F32))
            y_l, y_c = _retention(qkv_l, qkv_c, log_gamma, ret_gn_w[j], with_ctx=with_ctx)
            wg, wo = ret_wg[j].astype(BF16), ret_wo[j].astype(BF16)
            xl = _gated_out(xl, modr, i, nw1, wg, y_l, wo, rows_per_sample=SEQ)
            if with_ctx:
                xc = _gated_out(xc, modr, i, nw1, wg, y_c, wo, rows_per_sample=None)
        elif kind == 1:
            wq, wkv, wo = att_wq[j].astype(BF16), att_wkv[j].astype(BF16), att_wo[j].astype(BF16)
            n_k = ATT_KV_HEADS * ATT_HEAD_DIM
            q_l = _norm_proj(xl, modr, i, 0, nw1, wq, rows_per_sample=SEQ, out_dtype=BF16, mode="headnorm",
                             gain=att_q_gain[j], rope=rope, n_norm=COL_TILE, scale=ATT_Q_SCALE)
            kv_l = _norm_proj(xl, modr, i, 0, nw1, wkv, rows_per_sample=SEQ, out_dtype=BF16, mode="headnorm",
                              gain=att_k_gain[j], rope=rope, n_norm=n_k)
            kv_c = _norm_proj(xc, modr, i, 0, nw1, wkv, rows_per_sample=None, out_dtype=BF16, mode="headnorm",
                              gain=att_k_gain[j], n_norm=n_k)
            y_l = _attention(q_l, kv_c, kv_l)
            xl = _matmul_resid(y_l, wo, zero_bias, xl, modr, i, 0, rows_per_sample=SEQ)
            if with_ctx:
                q_c = _norm_proj(xc, modr, i, 0, nw1, wq, rows_per_sample=None, out_dtype=BF16, mode="headnorm",
                                 gain=att_q_gain[j], n_norm=COL_TILE, scale=ATT_Q_SCALE)
                y_c = _attention(q_c, kv_c, None)
                xc = _matmul_resid(y_c, wo, zero_bias, xc, modr, i, 0, rows_per_sample=None)
        else:
            w1, w2 = cnv_w1[j].astype(BF16), cnv_w2[j].astype(BF16)
            outs = []
            for xs, (rps, seq_len) in zip((xl, xc), streams()):
                if xs is xc and not with_ctx:
                    outs.append(xc)
                    continue
                a = _norm_proj(xs, modr, i, 0, nw1, w1, rows_per_sample=rps, out_dtype=F32, mode="glu",
                               bias=cnv_b1[j])
                a = _dwconv_ln_silu(a, cnv_dw[j], cnv_dw_b[j], cnv_ln_w[j], cnv_ln_b[j], seq_len=seq_len)
                outs.append(_matmul_resid(a, w2, cnv_b2[j], xs, modr, i, 0, rows_per_sample=rps))
            xl, xc = outs
        wg, wu, wd = ffn_w_gate[i].astype(BF16), ffn_w_up[i].astype(BF16), ffn_w_down[i].astype(BF16)
        xl = _conv_ffn(xl, modr, i, nw2, wg, wu, wd, ffn_dw[i], ffn_dw_b[i], rows_per_sample=SEQ, seq_len=SEQ)
        if with_ctx:
            xc = _conv_ffn(xc, modr, i, nw2, wg, wu, wd, ffn_dw[i], ffn_dw_b[i], rows_per_sample=None,
                           seq_len=CTX_LEN)
    return xl.reshape(BATCH, SEQ, D_MODEL)
```

```python
import functools

import jax
import jax.numpy as jnp
from jax import lax
from jax.experimental import pallas as pl
from jax.experimental.pallas import tpu as pltpu

D_MODEL = 2048
BATCH = 4
SEQ = 4096
DEPTH = 4
GRID_W = 64
CTX_LEN = 256
N_MIXERS = 3
RET_HEADS = 8
RET_QK_DIM = D_MODEL // RET_HEADS
RET_V_DIM = 2 * RET_QK_DIM
ATT_HEAD_DIM = 128
ATT_HEADS = D_MODEL // ATT_HEAD_DIM
ATT_KV_HEADS = 4
ATT_GROUPS = ATT_HEADS // ATT_KV_HEADS
ROPE_BASE = 10000.0
CONV_WIDTH = 31
FFN_DIM = ((8 * D_MODEL // 3 + 255) // 256) * 256
FFN_CONV_WIDTH = 3
N_MOD = 6
RMS_EPS = 1e-6
LN_EPS = 1e-5

ATT_Q_SCALE = ATT_HEAD_DIM ** -0.5 * 1.4426950408889634

BF16 = jnp.bfloat16
F32 = jnp.float32

LANES = 128
SUBLANES = 8
BF16_ROWS = 16
MXU_DIM = 256
VMEM_LIMIT_BYTES = 56 * 1024 * 1024

PROJ_ROW_TILE = 1024
FUSED_ROW_TILE = 512
COL_TILE = 1024
FUSED_COL_TILE = 512
GATE_COL_TILE = 1024
ATT_Q_TILE = 512
ATT_KV_CHUNK = 512
RET_SCAN_CHUNK = 256
CONV_ROW_TILE = 128
CONV_HALO = 16
CONV_COL_CHUNK = 256


def _tiles(extent, tile):
    assert extent % tile == 0, (extent, tile)
    return extent // tile


def _dot(a, b):
    return jnp.dot(a, b, preferred_element_type=F32)


def _dot_nt(a, b):
    return lax.dot_general(a, b, (((1,), (1,)), ((), ())), preferred_element_type=F32)


def _dot_tn(a, b):
    return lax.dot_general(a, b, (((0,), (0,)), ((), ())), preferred_element_type=F32)


def _sigmoid(x):
    return 1.0 / (1.0 + jnp.exp(-x))


def _silu(x):
    return x * _sigmoid(x)


def _modnorm(x, scale, shift):
    ms = jnp.mean(x * x, axis=-1, keepdims=True)
    return (x * lax.rsqrt(ms + RMS_EPS)) * scale + shift


def _fill_h(x_ref, nw_ref, sc_ref, sh_ref, h_ref):
    scale = nw_ref[...] * (1.0 + sc_ref[...])
    shift = sh_ref[...]
    for r in range(0, x_ref.shape[0], BF16_ROWS):
        rows = slice(r, r + BF16_ROWS)
        h_ref[rows, :] = _modnorm(x_ref[rows, :], scale, shift).astype(BF16)


def _params(*semantics):
    return pltpu.CompilerParams(dimension_semantics=semantics, vmem_limit_bytes=VMEM_LIMIT_BYTES)


def _slot(param, block, index):
    slot = param[1]
    return pl.BlockSpec((None,) + tuple(block), lambda *g: (slot,) + tuple(index(*g)))


def _mod_spec(layer, k, rows_per_sample, tm, tn=D_MODEL):
    col = (lambda j: 0) if tn == D_MODEL else (lambda j: j)
    if rows_per_sample is None:
        index = lambda i, j: (layer, k, BATCH, 0, col(j))
    else:
        tiles = _tiles(rows_per_sample, tm)
        index = lambda i, j: (layer, k, i // tiles, 0, col(j))
    return pl.BlockSpec((None, None, None, 1, tn), index)


def _row_spec(i, j):
    return (i, 0)


def _mod_body(c_ref, w_ref, b_ref, o_ref):
    c = c_ref[...]
    o_ref[...] = _dot(_silu(c).astype(BF16), w_ref[...].astype(BF16)) + b_ref[...]


def _modulation(cin, mod_w, mod_b):
    n = N_MOD * D_MODEL
    tn = COL_TILE
    return pl.pallas_call(
        _mod_body,
        out_shape=jax.ShapeDtypeStruct((DEPTH, SUBLANES, n), F32),
        grid=(DEPTH, _tiles(n, tn)),
        in_specs=[
            pl.BlockSpec((SUBLANES, D_MODEL), lambda l, j: (0, 0)),
            pl.BlockSpec((None, D_MODEL, tn), lambda l, j: (l, 0, j)),
            pl.BlockSpec((None, 1, tn), lambda l, j: (l, 0, j)),
        ],
        out_specs=pl.BlockSpec((None, SUBLANES, tn), lambda l, j: (l, 0, j)),
        compiler_params=_params("parallel", "parallel"),
        name="modulation",
    )(cin, mod_w, mod_b.reshape(DEPTH, 1, n))


def _proj_fill(x_ref, nw_ref, sc_ref, sh_ref, h_ref):
    @pl.when(pl.program_id(1) == 0)
    def _():
        _fill_h(x_ref, nw_ref, sc_ref, sh_ref, h_ref)


def _proj_body(x_ref, nw_ref, sc_ref, sh_ref, w_ref, o_ref, h_ref):
    _proj_fill(x_ref, nw_ref, sc_ref, sh_ref, h_ref)
    o_ref[...] = _dot(h_ref[...], w_ref[...]).astype(o_ref.dtype)


def _glu_body(x_ref, nw_ref, sc_ref, sh_ref, wa_ref, wb_ref, ba_ref, bb_ref, o_ref, h_ref):
    _proj_fill(x_ref, nw_ref, sc_ref, sh_ref, h_ref)
    h = h_ref[...]
    for c0 in range(0, o_ref.shape[1], MXU_DIM):
        cols = slice(c0, c0 + MXU_DIM)
        a = _dot(h, wa_ref[:, cols]) + ba_ref[:, cols]
        b = _dot(h, wb_ref[:, cols]) + bb_ref[:, cols]
        o_ref[:, cols] = (a * _sigmoid(b)).astype(o_ref.dtype)


def _headnorm_body(x_ref, nw_ref, sc_ref, sh_ref, w_ref, gain_ref, *rest, n_norm, scale, rope):
    if rope:
        cos_ref, sin_ref, o_ref, h_ref = rest
    else:
        o_ref, h_ref = rest
    _proj_fill(x_ref, nw_ref, sc_ref, sh_ref, h_ref)
    h = h_ref[...]
    hd = ATT_HEAD_DIM
    for c0 in range(0, o_ref.shape[1], MXU_DIM):
        acc = _dot(h, w_ref[:, c0:c0 + MXU_DIM])
        for h0 in range(0, MXU_DIM, hd):
            a = acc[:, h0:h0 + hd]
            if c0 + h0 < n_norm:
                ms = jnp.mean(a * a, axis=-1, keepdims=True)
                a = (a * lax.rsqrt(ms + RMS_EPS)) * gain_ref[...]
                if rope:
                    lane = lax.broadcasted_iota(jnp.int32, a.shape, 1)
                    quarter = hd // 4
                    partner = jnp.where((lane & (2 * quarter - 1)) < quarter,
                                        pltpu.roll(a, hd - quarter, 1), pltpu.roll(a, quarter, 1))
                    a = a * cos_ref[...] + partner * sin_ref[...]
                if scale != 1.0:
                    a = a * scale
            o_ref[:, c0 + h0:c0 + h0 + hd] = a.astype(o_ref.dtype)


def _norm_proj(x, modr, layer, mod_base, nw, w, *, rows_per_sample, out_dtype, mode="plain",
               bias=None, gain=None, rope=None, n_norm=0, scale=1.0):
    rows = x.shape[0]
    tm, tn = PROJ_ROW_TILE, COL_TILE
    n_w = w[0].shape[2]
    n_out = _tiles(n_w, 2) if mode == "glu" else n_w
    w_spec = lambda off: _slot(w, (D_MODEL, tn), lambda i, j: (0, j + off))
    specs = [
        pl.BlockSpec((tm, D_MODEL), _row_spec),
        _slot(nw, (1, D_MODEL), lambda i, j: (0, 0)),
        _mod_spec(layer, mod_base + 1, rows_per_sample, tm),
        _mod_spec(layer, mod_base + 0, rows_per_sample, tm),
    ]
    args = [x, nw[0], modr, modr]
    if mode == "plain":
        body = _proj_body
        specs += [w_spec(0)]
        args += [w[0]]
    elif mode == "glu":
        body = _glu_body
        off = _tiles(n_out, tn)
        b_spec = lambda o: _slot(bias, (1, tn), lambda i, j: (0, j + o))
        specs += [w_spec(0), w_spec(off), b_spec(0), b_spec(off)]
        args += [w[0], w[0], bias[0], bias[0]]
    else:
        body = functools.partial(_headnorm_body, n_norm=n_norm, scale=scale, rope=rope is not None)
        specs += [w_spec(0), _slot(gain, (1, ATT_HEAD_DIM), lambda i, j: (0, 0))]
        args += [w[0], gain[0]]
        if rope is not None:
            seq_tiles = _tiles(SEQ, tm)
            specs += [pl.BlockSpec((tm, ATT_HEAD_DIM), lambda i, j: (i % seq_tiles, 0))] * 2
            args += list(rope)
    return pl.pallas_call(
        body,
        out_shape=jax.ShapeDtypeStruct((rows, n_out), out_dtype),
        grid=(_tiles(rows, tm), _tiles(n_out, tn)),
        in_specs=specs,
        out_specs=pl.BlockSpec((tm, tn), lambda i, j: (i, j)),
        scratch_shapes=[pltpu.VMEM((tm, D_MODEL), BF16)],
        compiler_params=_params("parallel", "arbitrary"),
        name="norm_proj_" + mode,
    )(*args)


def _resid_body(a_ref, w_ref, *rest, has_bias):
    if has_bias:
        b_ref, x_ref, g_ref, o_ref = rest
    else:
        x_ref, g_ref, o_ref = rest
    acc = _dot(a_ref[...], w_ref[...])
    if has_bias:
        acc = acc + b_ref[...]
    o_ref[...] = x_ref[...] + g_ref[...] * acc


def _matmul_resid(a, w, bias, x, modr, layer, mod_base, *, rows_per_sample):
    rows, k = a.shape
    tm, tn = PROJ_ROW_TILE, COL_TILE
    specs = [pl.BlockSpec((tm, k), _row_spec), _slot(w, (k, tn), lambda i, j: (0, j))]
    args = [a, w[0]]
    if bias is not None:
        specs += [_slot(bias, (1, tn), lambda i, j: (0, j))]
        args += [bias[0]]
    specs += [pl.BlockSpec((tm, tn), lambda i, j: (i, j)),
              _mod_spec(layer, mod_base + 2, rows_per_sample, tm, tn)]
    args += [x, modr]
    return pl.pallas_call(
        functools.partial(_resid_body, has_bias=bias is not None),
        out_shape=jax.ShapeDtypeStruct((rows, D_MODEL), F32),
        grid=(_tiles(rows, tm), _tiles(D_MODEL, tn)),
        in_specs=specs,
        out_specs=pl.BlockSpec((tm, tn), lambda i, j: (i, j)),
        compiler_params=_params("parallel", "parallel"),
        name="matmul_resid",
    )(*args)


def _fused_steps(x_ref, nw_ref, sc_ref, sh_ref, g_ref, o_ref, h_ref, hidden, w_out_ref):
    def product():
        return g_ref[...] * _dot(hidden(), w_out_ref[...])

    @pl.when(pl.program_id(1) == 0)
    def _():
        _fill_h(x_ref, nw_ref, sc_ref, sh_ref, h_ref)
        o_ref[...] = x_ref[...] + product()

    @pl.when(pl.program_id(1) > 0)
    def _():
        o_ref[...] += product()


def _ffn_body(x_ref, nw_ref, sc_ref, sh_ref, g_ref, wg_ref, wu_ref, wd_ref, dw_ref, dwb_ref, halo_ref,
              o_ref, h_ref, *, seq_len):
    tm = x_ref.shape[0]

    def hidden():
        h = h_ref[...]
        gate = _dot(h, wg_ref[...])
        up = _dot(h, wu_ref[...])
        row = lax.broadcasted_iota(jnp.int32, gate.shape, 0)
        pos = (row + pl.program_id(0) * tm) & (seq_len - 1)
        prev = jnp.where(row == 0, halo_ref[0:1, :], pltpu.roll(gate, 1, 0))
        prev = jnp.where(pos == 0, 0.0, prev)
        nxt = jnp.where(row == tm - 1, halo_ref[1:2, :], pltpu.roll(gate, tm - 1, 0))
        nxt = jnp.where(pos == seq_len - 1, 0.0, nxt)
        conv = prev * dw_ref[0:1, :] + gate * dw_ref[1:2, :] + nxt * dw_ref[2:3, :] + dwb_ref[...]
        return (_silu(conv) * up).astype(BF16)

    _fused_steps(x_ref, nw_ref, sc_ref, sh_ref, g_ref, o_ref, h_ref, hidden, wd_ref)


def _gate_body(x_ref, nw_ref, sc_ref, sh_ref, g_ref, wg_ref, y_ref, wo_ref, o_ref, h_ref):
    def hidden():
        return (_silu(_dot(h_ref[...], wg_ref[...])) * y_ref[...].astype(F32)).astype(BF16)

    _fused_steps(x_ref, nw_ref, sc_ref, sh_ref, g_ref, o_ref, h_ref, hidden, wo_ref)


def _halo_body(x_ref, nw_ref, sc_ref, sh_ref, w_ref, o_ref, *, rows_per_sample_block):
    m = x_ref.shape[0]
    if rows_per_sample_block is None:
        sc, sh = sc_ref[BATCH:BATCH + 1, :], sh_ref[BATCH:BATCH + 1, :]
    else:
        blocks = _tiles(m, rows_per_sample_block)
        pick = lambda ref: jnp.concatenate(
            [jnp.broadcast_to(ref[b:b + 1, :], (rows_per_sample_block, D_MODEL)) for b in range(blocks)], axis=0)
        sc, sh = pick(sc_ref), pick(sh_ref)
    h = _modnorm(x_ref[...], nw_ref[...] * (1.0 + sc), sh).astype(BF16)
    o_ref[...] = _dot(h, w_ref[...])


def _ffn_halo(x, mod, layer, nw, wg, *, rows_per_sample):
    rows = x.shape[0]
    tm, tn = FUSED_ROW_TILE, FUSED_COL_TILE
    nt = _tiles(rows, tm)
    t = jnp.arange(nt, dtype=jnp.int32)
    prev = jnp.maximum(t * tm - 1, 0)
    nxt = jnp.minimum((t + 1) * tm, rows - 1)
    idx = jnp.stack([prev, nxt] + [prev] * (SUBLANES - 2), axis=1).reshape(-1)
    xh = x[idx]
    m = nt * SUBLANES
    f = wg[0].shape[2]
    per_sample = None if rows_per_sample is None else SUBLANES * _tiles(rows_per_sample, tm)
    mod_slab = lambda k: pl.BlockSpec((None, SUBLANES, D_MODEL), lambda j: (layer, 0, k))
    out = pl.pallas_call(
        functools.partial(_halo_body, rows_per_sample_block=per_sample),
        out_shape=jax.ShapeDtypeStruct((m, f), F32),
        grid=(_tiles(f, tn),),
        in_specs=[
            pl.BlockSpec((m, D_MODEL), lambda j: (0, 0)),
            _slot(nw, (1, D_MODEL), lambda j: (0, 0)),
            mod_slab(4),
            mod_slab(3),
            _slot(wg, (D_MODEL, tn), lambda j: (0, j)),
        ],
        out_specs=pl.BlockSpec((m, tn), lambda j: (0, j)),
        compiler_params=_params("parallel"),
        name="ffn_halo",
    )(xh, nw[0], mod, mod, wg[0])
    return out.reshape(nt, SUBLANES, f)


def _fused_head(x, modr, layer, mod_base, nw, rows_per_sample):
    tm = FUSED_ROW_TILE
    specs = [
        pl.BlockSpec((tm, D_MODEL), _row_spec),
        _slot(nw, (1, D_MODEL), lambda i, c: (0, 0)),
        _mod_spec(layer, mod_base + 1, rows_per_sample, tm),
        _mod_spec(layer, mod_base + 0, rows_per_sample, tm),
        _mod_spec(layer, mod_base + 2, rows_per_sample, tm),
    ]
    return specs, [x, nw[0], modr, modr, modr]


def _conv_ffn(x, mod, modr, layer, nw, wg, wu, wd, dw, dwb, *, rows_per_sample, seq_len):
    rows = x.shape[0]
    tm, tf = FUSED_ROW_TILE, FUSED_COL_TILE
    f = wg[0].shape[2]
    halo = _ffn_halo(x, mod, layer, nw, wg, rows_per_sample=rows_per_sample)
    specs, args = _fused_head(x, modr, layer, 3, nw, rows_per_sample)
    specs += [
        _slot(wg, (D_MODEL, tf), lambda i, c: (0, c)),
        _slot(wu, (D_MODEL, tf), lambda i, c: (0, c)),
        _slot(wd, (tf, D_MODEL), lambda i, c: (c, 0)),
        _slot(dw, (FFN_CONV_WIDTH, tf), lambda i, c: (0, c)),
        _slot(dwb, (1, tf), lambda i, c: (0, c)),
        pl.BlockSpec((None, SUBLANES, tf), lambda i, c: (i, 0, c)),
    ]
    args += [wg[0], wu[0], wd[0], dw[0], dwb[0], halo]
    return pl.pallas_call(
        functools.partial(_ffn_body, seq_len=seq_len),
        out_shape=jax.ShapeDtypeStruct((rows, D_MODEL), F32),
        grid=(_tiles(rows, tm), _tiles(f, tf)),
        in_specs=specs,
        out_specs=pl.BlockSpec((tm, D_MODEL), _row_spec),
        scratch_shapes=[pltpu.VMEM((tm, D_MODEL), BF16)],
        compiler_params=_params("parallel", "arbitrary"),
        name="conv_ffn",
    )(*args)


def _gated_out(x, modr, layer, nw, wg, y, wo, *, rows_per_sample):
    rows = x.shape[0]
    tm, tf = FUSED_ROW_TILE, GATE_COL_TILE
    f = wg[0].shape[2]
    specs, args = _fused_head(x, modr, layer, 0, nw, rows_per_sample)
    specs += [
        _slot(wg, (D_MODEL, tf), lambda i, c: (0, c)),
        pl.BlockSpec((tm, tf), lambda i, c: (i, c)),
        _slot(wo, (tf, D_MODEL), lambda i, c: (c, 0)),
    ]
    args += [wg[0], y, wo[0]]
    return pl.pallas_call(
        _gate_body,
        out_shape=jax.ShapeDtypeStruct((rows, D_MODEL), F32),
        grid=(_tiles(rows, tm), _tiles(f, tf)),
        in_specs=specs,
        out_specs=pl.BlockSpec((tm, D_MODEL), _row_spec),
        scratch_shapes=[pltpu.VMEM((tm, D_MODEL), BF16)],
        compiler_params=_params("parallel", "arbitrary"),
        name="gated_out",
    )(*args)


def _ret_body(lg_ref, ql_ref, kl_ref, vl_ref, qc_ref, kc_ref, vc_ref, gn_ref, yl_ref, *rest, with_ctx):
    if with_ctx:
        yc_ref, sf_ref, sb_ref, part_ref = rest
    else:
        sf_ref, sb_ref, part_ref = rest
    head = pl.program_id(1)
    lgf = lg_ref[0, head]
    lgb = lg_ref[1, head]
    c = RET_SCAN_CHUNK
    cf = float(c)
    n = lax.broadcasted_iota(jnp.int32, (c, c), 0).astype(F32)
    m = lax.broadcasted_iota(jnp.int32, (c, c), 1).astype(F32)
    diff = n - m
    intra = (jnp.where(diff >= 0, jnp.exp(lgf * jnp.maximum(diff, 0.0)), 0.0)
             + jnp.where(diff <= 0, jnp.exp(lgb * jnp.maximum(-diff, 0.0)), 0.0))
    pv = lax.broadcasted_iota(jnp.int32, (c, RET_V_DIM), 0).astype(F32)
    pk = lax.broadcasted_iota(jnp.int32, (c, RET_QK_DIM), 0).astype(F32)
    q_dec_f = jnp.exp(lgf * (pv + 1.0))
    q_dec_b = jnp.exp(lgb * (cf - pv))
    k_dec_f = jnp.exp(lgf * (cf - 1.0 - pk))
    k_dec_b = jnp.exp(lgb * pk)
    chunk_dec_f = jnp.exp(lgf * jnp.full((1, RET_V_DIM), cf, F32))
    chunk_dec_b = jnp.exp(lgb * jnp.full((1, RET_V_DIM), cf, F32))

    def group_norm(y):
        mu = jnp.mean(y, axis=-1, keepdims=True)
        yc = y - mu
        var = jnp.mean(yc * yc, axis=-1, keepdims=True)
        return (yc * lax.rsqrt(var + LN_EPS)) * gn_ref[...]

    def intra_term(q, k, v):
        return _dot((_dot_nt(q, k) * intra).astype(BF16), v)

    def key_state(k, v, k_dec):
        return _dot_tn((k.astype(F32) * k_dec).astype(BF16), v)

    assert CTX_LEN == c
    qc, kc, vc = qc_ref[...], kc_ref[...], vc_ref[...]
    if with_ctx:
        yc_ref[...] = group_norm(intra_term(qc, kc, vc)).astype(yc_ref.dtype)
    sf_ref[...] = key_state(kc, vc, k_dec_f)
    sb_ref[...] = key_state(kc, vc, k_dec_b)

    n_lat = _tiles(SEQ, c)
    half = _tiles(n_lat, 2)

    def finish(rows, y, second):
        if second:
            yl_ref[rows, :] = group_norm(y + part_ref[rows, :]).astype(yl_ref.dtype)
        else:
            part_ref[rows, :] = y

    def visit(j, second):
        rows_f = pl.ds(pl.multiple_of(j * c, c), c)
        rows_b = pl.ds(pl.multiple_of((n_lat - 1 - j) * c, c), c)
        q, k, v = ql_ref[rows_f, :], kl_ref[rows_f, :], vl_ref[rows_f, :]
        finish(rows_f, intra_term(q, k, v) + _dot(q, sf_ref[...].astype(BF16)) * q_dec_f, second)
        sf_ref[...] = sf_ref[...] * chunk_dec_f + key_state(k, v, k_dec_f)
        q, k, v = ql_ref[rows_b, :], kl_ref[rows_b, :], vl_ref[rows_b, :]
        finish(rows_b, _dot(q, sb_ref[...].astype(BF16)) * q_dec_b, second)
        sb_ref[...] = sb_ref[...] * chunk_dec_b + key_state(k, v, k_dec_b)

    def first_half(j, carry):
        visit(j, False)
        return carry

    def second_half(j, carry):
        visit(j, True)
        return carry

    lax.fori_loop(0, half, first_half, 0)
    lax.fori_loop(half, n_lat, second_half, 0)


def _retention(qkv_l, qkv_c, log_gamma, gn_w, *, with_ctx):
    dk, dv, h = RET_QK_DIM, RET_V_DIM, RET_HEADS
    hv = h * dv
    k_off = h
    v_off = _tiles(2 * h * dk, dv)
    in_specs = [
        pl.BlockSpec((SEQ, dk), lambda b, n, lg: (b, n)),
        pl.BlockSpec((SEQ, dk), lambda b, n, lg: (b, k_off + n)),
        pl.BlockSpec((SEQ, dv), lambda b, n, lg: (b, v_off + n)),
        pl.BlockSpec((CTX_LEN, dk), lambda b, n, lg: (b, n)),
        pl.BlockSpec((CTX_LEN, dk), lambda b, n, lg: (b, k_off + n)),
        pl.BlockSpec((CTX_LEN, dv), lambda b, n, lg: (b, v_off + n)),
        _slot(gn_w, (1, dv), lambda b, n, lg: (0, n)),
    ]
    out_shape = [jax.ShapeDtypeStruct((BATCH * SEQ, hv), BF16)]
    out_specs = [pl.BlockSpec((SEQ, dv), lambda b, n, lg: (b, n))]
    if with_ctx:
        out_shape.append(jax.ShapeDtypeStruct((BATCH * CTX_LEN, hv), BF16))
        out_specs.append(pl.BlockSpec((CTX_LEN, dv), lambda b, n, lg: (b, n)))
    outs = pl.pallas_call(
        functools.partial(_ret_body, with_ctx=with_ctx),
        out_shape=out_shape,
        grid_spec=pltpu.PrefetchScalarGridSpec(
            num_scalar_prefetch=1,
            grid=(BATCH, h),
            in_specs=in_specs,
            out_specs=out_specs,
            scratch_shapes=[pltpu.VMEM((dk, dv), F32), pltpu.VMEM((dk, dv), F32), pltpu.VMEM((SEQ, dv), F32)],
        ),
        compiler_params=_params("parallel", "parallel"),
        name="retention",
    )(log_gamma, qkv_l, qkv_l, qkv_l, qkv_c, qkv_c, qkv_c, gn_w[0])
    return (outs[0], outs[1]) if with_ctx else (outs[0], None)


def _att_body(q_ref, kc_ref, vc_ref, *rest, with_lat):
    if with_lat:
        kl_ref, vl_ref, o_ref, vtc_ref, vtl_ref = rest
    else:
        o_ref, vtc_ref = rest
    hd = ATT_HEAD_DIM
    tq = q_ref.shape[0]
    kvc = ATT_KV_CHUNK

    @pl.when(pl.program_id(2) == 0)
    def _():
        vtc_ref[...] = vc_ref[...].astype(F32).T.astype(BF16)
        if with_lat:
            for c in range(_tiles(SEQ, kvc)):
                vtl_ref[:, c * kvc:(c + 1) * kvc] = vl_ref[c * kvc:(c + 1) * kvc, :].astype(F32).T.astype(BF16)

    q_all = jnp.concatenate([q_ref[:, g * hd:(g + 1) * hd] for g in range(ATT_GROUPS)], axis=0)
    chunks = [(kc_ref, vtc_ref, 0, CTX_LEN)]
    if with_lat:
        chunks += [(kl_ref, vtl_ref, c * kvc, kvc) for c in range(_tiles(SEQ, kvc))]

    def scores(chunk):
        k_ref, _, r0, n = chunk
        return _dot_nt(k_ref[r0:r0 + n, :], q_all)

    def weighted_values(chunk, p):
        _, vt_ref, r0, n = chunk
        return _dot(vt_ref[:, r0:r0 + n], p)

    mx = den = acc = None
    pending = None
    s_next = scores(chunks[0])
    for idx, chunk in enumerate(chunks):
        s = s_next
        if idx + 1 < len(chunks):
            s_next = scores(chunks[idx + 1])
        if pending is not None:
            pv = weighted_values(pending[0], pending[1])
            acc = pv if acc is None else acc * pending[2] + pv
        mx_new = jnp.max(s, axis=0, keepdims=True)
        alpha = None
        if mx is not None:
            mx_new = jnp.maximum(mx, mx_new)
            alpha = jnp.exp2(mx - mx_new)
        p = jnp.exp2(s - mx_new)
        p_sum = jnp.sum(p, axis=0, keepdims=True)
        den = p_sum if den is None else den * alpha + p_sum
        mx = mx_new
        pending = (chunk, p.astype(BF16), alpha)
    pv = weighted_values(pending[0], pending[1])
    acc = pv if acc is None else acc * pending[2] + pv
    out = acc / den
    for g in range(ATT_GROUPS):
        o_ref[:, g * hd:(g + 1) * hd] = out[:, g * tq:(g + 1) * tq].T.astype(o_ref.dtype)


def _attention(q, kv_c, kv_l):
    hd, kvh = ATT_HEAD_DIM, ATT_KV_HEADS
    gw = ATT_GROUPS * hd
    n_q = CTX_LEN if kv_l is None else SEQ
    tq = min(ATT_Q_TILE, n_q)
    qt = _tiles(n_q, tq)
    q_spec = pl.BlockSpec((tq, gw), lambda b, n, t: (b * qt + t, n))
    specs = [q_spec,
             pl.BlockSpec((CTX_LEN, hd), lambda b, n, t: (b, n)),
             pl.BlockSpec((CTX_LEN, hd), lambda b, n, t: (b, kvh + n))]
    args = [q, kv_c, kv_c]
    scratch = [pltpu.VMEM((hd, CTX_LEN), BF16)]
    if kv_l is not None:
        specs += [pl.BlockSpec((SEQ, hd), lambda b, n, t: (b, n)),
                  pl.BlockSpec((SEQ, hd), lambda b, n, t: (b, kvh + n))]
        args += [kv_l, kv_l]
        scratch += [pltpu.VMEM((hd, SEQ), BF16)]
    return pl.pallas_call(
        functools.partial(_att_body, with_lat=kv_l is not None),
        out_shape=jax.ShapeDtypeStruct(q.shape, BF16),
        grid=(BATCH, kvh, qt),
        in_specs=specs,
        out_specs=q_spec,
        scratch_shapes=scratch,
        compiler_params=_params("parallel", "parallel", "arbitrary"),
        name="attention",
    )(*args)


def _dwconv_body(ap_ref, a_ref, an_ref, dw_ref, dwb_ref, lnw_ref, lnb_ref, o_ref, buf_ref, cv_ref, *, seq_tiles):
    tm = a_ref.shape[0]
    pos = pl.program_id(0) % seq_tiles
    buf_ref[0:CONV_HALO, :] = jnp.where(pos == 0, 0.0, ap_ref[...])
    buf_ref[CONV_HALO:CONV_HALO + tm, :] = a_ref[...]
    buf_ref[CONV_HALO + tm:, :] = jnp.where(pos == seq_tiles - 1, 0.0, an_ref[...])
    first = CONV_HALO - CONV_WIDTH // 2
    rows = tm + 2 * CONV_HALO
    for cc in range(_tiles(D_MODEL, CONV_COL_CHUNK)):
        cols = slice(cc * CONV_COL_CHUNK, (cc + 1) * CONV_COL_CHUNK)
        acc = jnp.broadcast_to(dwb_ref[:, cols], (tm, CONV_COL_CHUNK))
        window = buf_ref[:, cols]
        for shift in range(SUBLANES):
            taps = [d for d in range(CONV_WIDTH) if (first + d) % SUBLANES == shift]
            shifted = window if shift == 0 else pltpu.roll(window, rows - shift, 0)
            for d in taps:
                r0 = first + d - shift
                acc = acc + shifted[r0:r0 + tm, :] * dw_ref[d:d + 1, cols]
        cv_ref[:, cols] = acc
    y = cv_ref[...]
    mu = jnp.mean(y, axis=-1, keepdims=True)
    yc = y - mu
    var = jnp.mean(yc * yc, axis=-1, keepdims=True)
    z = (yc * lax.rsqrt(var + LN_EPS)) * lnw_ref[...] + lnb_ref[...]
    o_ref[...] = _silu(z).astype(o_ref.dtype)


def _dwconv_ln_silu(a, dw, dwb, lnw, lnb, *, seq_len):
    rows = a.shape[0]
    tm = CONV_ROW_TILE
    per = _tiles(tm, CONV_HALO)
    last = _tiles(rows, CONV_HALO) - 1
    vec = lambda p: _slot(p, (1, D_MODEL), lambda i: (0, 0))
    return pl.pallas_call(
        functools.partial(_dwconv_body, seq_tiles=_tiles(seq_len, tm)),
        out_shape=jax.ShapeDtypeStruct((rows, D_MODEL), BF16),
        grid=(_tiles(rows, tm),),
        in_specs=[
            pl.BlockSpec((CONV_HALO, D_MODEL), lambda i: (jnp.maximum(i * per - 1, 0), 0)),
            pl.BlockSpec((tm, D_MODEL), lambda i: (i, 0)),
            pl.BlockSpec((CONV_HALO, D_MODEL), lambda i: (jnp.minimum((i + 1) * per, last), 0)),
            _slot(dw, (CONV_WIDTH, D_MODEL), lambda i: (0, 0)),
            vec(dwb), vec(lnw), vec(lnb),
        ],
        out_specs=pl.BlockSpec((tm, D_MODEL), lambda i: (i, 0)),
        scratch_shapes=[pltpu.VMEM((tm + 2 * CONV_HALO, D_MODEL), F32), pltpu.VMEM((tm, D_MODEL), F32)],
        compiler_params=_params("parallel"),
        name="dwconv_ln_silu",
    )(a, a, a, dw[0], dwb[0], lnw[0], lnb[0])


def _rope_tables():
    nf = ATT_HEAD_DIM // 4
    t = jnp.arange(SEQ)
    inv_freq = ROPE_BASE ** (-jnp.arange(nf, dtype=F32) / nf)
    ang_r = (t // GRID_W).astype(F32)[:, None] * inv_freq[None, :]
    ang_c = (t % GRID_W).astype(F32)[:, None] * inv_freq[None, :]
    cos = jnp.concatenate([jnp.cos(ang_r)] * 2 + [jnp.cos(ang_c)] * 2, axis=-1)
    sin = jnp.concatenate([-jnp.sin(ang_r), jnp.sin(ang_r), -jnp.sin(ang_c), jnp.sin(ang_c)], axis=-1)
    return cos, sin


def _rows(v):
    return v.reshape(v.shape[0], 1, v.shape[1])


def kernel(x, c, ctx, c_ctx, mod_w, mod_b, norm1_w, norm2_w, ffn_w_gate, ffn_w_up, ffn_dw, ffn_dw_b, ffn_w_down, ret_wq, ret_wk, ret_wv, ret_wg, ret_wo, ret_decay, ret_gn_w, att_wq, att_wkv, att_q_gain, att_k_gain, att_wo, cnv_w1, cnv_b1, cnv_dw, cnv_dw_b, cnv_ln_w, cnv_ln_b, cnv_w2, cnv_b2):
    xl = x.reshape(BATCH * SEQ, D_MODEL)
    xc = ctx.reshape(BATCH * CTX_LEN, D_MODEL)

    cin = jnp.concatenate([c, c_ctx[None, :], jnp.zeros((SUBLANES - BATCH - 1, D_MODEL), F32)], axis=0)
    mod = _modulation(cin, mod_w, mod_b)
    modr = mod.reshape(DEPTH, SUBLANES, N_MOD, D_MODEL).transpose(0, 2, 1, 3)[:, :, :, None, :]
    rope = _rope_tables()

    bf = lambda w: w.astype(BF16)
    norm1, norm2 = _rows(norm1_w), _rows(norm2_w)
    ffn_wg, ffn_wu, ffn_wd, ffn_b = bf(ffn_w_gate), bf(ffn_w_up), bf(ffn_w_down), _rows(ffn_dw_b)
    ret_qkv = bf(jnp.concatenate([ret_wq, ret_wk * (RET_QK_DIM ** -0.5), ret_wv], axis=-1))
    ret_g, ret_o, ret_gn = bf(ret_wg), bf(ret_wo), _rows(ret_gn_w)
    ret_lg = -jnp.exp(ret_decay.astype(F32))
    att_q, att_kv, att_o = bf(att_wq), bf(att_wkv), bf(att_wo)
    att_qg, att_kg = _rows(att_q_gain), _rows(att_k_gain)
    cnv_1, cnv_2 = bf(cnv_w1), bf(cnv_w2)
    cnv_b1r, cnv_b2r, cnv_dwb, cnv_lnw, cnv_lnb = map(_rows, (cnv_b1, cnv_b2, cnv_dw_b, cnv_ln_w, cnv_ln_b))

    for i in range(DEPTH):
        with_ctx = i < DEPTH - 1
        kind, j = i % N_MIXERS, i // N_MIXERS
        nw1, nw2 = (norm1, i), (norm2, i)
        if kind == 0:
            qkv_l = _norm_proj(xl, modr, i, 0, nw1, (ret_qkv, j), rows_per_sample=SEQ, out_dtype=BF16)
            qkv_c = _norm_proj(xc, modr, i, 0, nw1, (ret_qkv, j), rows_per_sample=None, out_dtype=BF16)
            y_l, y_c = _retention(qkv_l, qkv_c, ret_lg[j], (ret_gn, j), with_ctx=with_ctx)
            xl = _gated_out(xl, modr, i, nw1, (ret_g, j), y_l, (ret_o, j), rows_per_sample=SEQ)
            if with_ctx:
                xc = _gated_out(xc, modr, i, nw1, (ret_g, j), y_c, (ret_o, j), rows_per_sample=None)
        elif kind == 1:
            n_k = ATT_KV_HEADS * ATT_HEAD_DIM
            q_args = dict(out_dtype=BF16, mode="headnorm", gain=(att_qg, j), n_norm=D_MODEL, scale=ATT_Q_SCALE)
            kv_args = dict(out_dtype=BF16, mode="headnorm", gain=(att_kg, j), n_norm=n_k)
            q_l = _norm_proj(xl, modr, i, 0, nw1, (att_q, j), rows_per_sample=SEQ, rope=rope, **q_args)
            kv_l = _norm_proj(xl, modr, i, 0, nw1, (att_kv, j), rows_per_sample=SEQ, rope=rope, **kv_args)
            kv_c = _norm_proj(xc, modr, i, 0, nw1, (att_kv, j), rows_per_sample=None, **kv_args)
            y_l = _attention(q_l, kv_c, kv_l)
            xl = _matmul_resid(y_l, (att_o, j), None, xl, modr, i, 0, rows_per_sample=SEQ)
            if with_ctx:
                q_c = _norm_proj(xc, modr, i, 0, nw1, (att_q, j), rows_per_sample=None, **q_args)
                y_c = _attention(q_c, kv_c, None)
                xc = _matmul_resid(y_c, (att_o, j), None, xc, modr, i, 0, rows_per_sample=None)
        else:
            def conformer(xs, rows_per_sample, seq_len):
                a = _norm_proj(xs, modr, i, 0, nw1, (cnv_1, j), rows_per_sample=rows_per_sample, out_dtype=F32,
                               mode="glu", bias=(cnv_b1r, j))
                a = _dwconv_ln_silu(a, (cnv_dw, j), (cnv_dwb, j), (cnv_lnw, j), (cnv_lnb, j), seq_len=seq_len)
                return _matmul_resid(a, (cnv_2, j), (cnv_b2r, j), xs, modr, i, 0, rows_per_sample=rows_per_sample)

            xl = conformer(xl, SEQ, SEQ)
            if with_ctx:
                xc = conformer(xc, None, CTX_LEN)
        ffn = functools.partial(_conv_ffn, mod=mod, modr=modr, layer=i, nw=nw2, wg=(ffn_wg, i), wu=(ffn_wu, i),
                                wd=(ffn_wd, i), dw=(ffn_dw, i), dwb=(ffn_b, i))
        xl = ffn(xl, rows_per_sample=SEQ, seq_len=SEQ)
        if with_ctx:
            xc = ffn(xc, rows_per_sample=None, seq_len=CTX_LEN)
    return xl.reshape(BATCH, SEQ, D_MODEL)
```

```python
import functools

import jax
import jax.numpy as jnp
from jax import lax
from jax.experimental import pallas as pl
from jax.experimental.pallas import tpu as pltpu

D_MODEL = 2048
BATCH = 4
SEQ = 4096
DEPTH = 4
GRID_W = 64
CTX_LEN = 256
N_MIXERS = 3
RET_HEADS = 8
RET_QK_DIM = D_MODEL // RET_HEADS
RET_V_DIM = 2 * RET_QK_DIM
ATT_HEAD_DIM = 128
ATT_HEADS = D_MODEL // ATT_HEAD_DIM
ATT_KV_HEADS = 4
ATT_GROUPS = ATT_HEADS // ATT_KV_HEADS
ROPE_BASE = 10000.0
CONV_WIDTH = 31
FFN_DIM = ((8 * D_MODEL // 3 + 255) // 256) * 256
FFN_CONV_WIDTH = 3
N_MOD = 6
RMS_EPS = 1e-6
LN_EPS = 1e-5

ATT_Q_SCALE = ATT_HEAD_DIM ** -0.5 * 1.4426950408889634

BF16 = jnp.bfloat16
F32 = jnp.float32

LANES = 128
SUBLANES = 8
BF16_ROWS = 16
MXU_DIM = 256
VMEM_LIMIT_BYTES = 56 * 1024 * 1024

PROJ_ROW_TILE = 1024
FUSED_ROW_TILE = 512
COL_TILE = 1024
FUSED_COL_TILE = 512
GATE_COL_TILE = 1024
ATT_Q_TILE = 512
ATT_KV_CHUNK = 512
RET_SCAN_CHUNK = 256
CONV_ROW_TILE = 128
CONV_HALO = 16
CONV_COL_CHUNK = 256


def _tiles(extent, tile):
    assert extent % tile == 0, (extent, tile)
    return extent // tile


def _dot(a, b):
    return jnp.dot(a, b, preferred_element_type=F32)


def _dot_nt(a, b):
    return lax.dot_general(a, b, (((1,), (1,)), ((), ())), preferred_element_type=F32)


def _dot_tn(a, b):
    return lax.dot_general(a, b, (((0,), (0,)), ((), ())), preferred_element_type=F32)


def _sigmoid(x):
    return 1.0 / (1.0 + jnp.exp(-x))


def _silu(x):
    return x * _sigmoid(x)


def _modnorm(x, scale, shift):
    ms = jnp.mean(x * x, axis=-1, keepdims=True)
    return (x * lax.rsqrt(ms + RMS_EPS)) * scale + shift


def _fill_h(x_ref, nw_ref, sc_ref, sh_ref, h_ref):
    scale = nw_ref[...] * (1.0 + sc_ref[...])
    shift = sh_ref[...]
    for r in range(0, x_ref.shape[0], BF16_ROWS):
        rows = slice(r, r + BF16_ROWS)
        h_ref[rows, :] = _modnorm(x_ref[rows, :], scale, shift).astype(BF16)


def _params(*semantics):
    return pltpu.CompilerParams(dimension_semantics=semantics, vmem_limit_bytes=VMEM_LIMIT_BYTES)


def _slot(param, block, index):
    slot = param[1]
    return pl.BlockSpec((None,) + tuple(block), lambda *g: (slot,) + tuple(index(*g)))


def _mod_spec(layer, k, rows_per_sample, tm, tn=D_MODEL):
    col = (lambda j: 0) if tn == D_MODEL else (lambda j: j)
    if rows_per_sample is None:
        index = lambda i, j: (layer, k, BATCH, 0, col(j))
    else:
        tiles = _tiles(rows_per_sample, tm)
        index = lambda i, j: (layer, k, i // tiles, 0, col(j))
    return pl.BlockSpec((None, None, None, 1, tn), index)


def _row_spec(i, j):
    return (i, 0)


def _mod_body(c_ref, w_ref, b_ref, o_ref):
    c = c_ref[...]
    o_ref[...] = _dot(_silu(c).astype(BF16), w_ref[...].astype(BF16)) + b_ref[...]


def _modulation(cin, mod_w, mod_b):
    n = N_MOD * D_MODEL
    tn = COL_TILE
    return pl.pallas_call(
        _mod_body,
        out_shape=jax.ShapeDtypeStruct((DEPTH, SUBLANES, n), F32),
        grid=(DEPTH, _tiles(n, tn)),
        in_specs=[
            pl.BlockSpec((SUBLANES, D_MODEL), lambda l, j: (0, 0)),
            pl.BlockSpec((None, D_MODEL, tn), lambda l, j: (l, 0, j)),
            pl.BlockSpec((None, 1, tn), lambda l, j: (l, 0, j)),
        ],
        out_specs=pl.BlockSpec((None, SUBLANES, tn), lambda l, j: (l, 0, j)),
        compiler_params=_params("parallel", "parallel"),
        name="modulation",
    )(cin, mod_w, mod_b.reshape(DEPTH, 1, n))


def _proj_fill(x_ref, nw_ref, sc_ref, sh_ref, h_ref):
    @pl.when(pl.program_id(1) == 0)
    def _():
        _fill_h(x_ref, nw_ref, sc_ref, sh_ref, h_ref)


def _proj_body(x_ref, nw_ref, sc_ref, sh_ref, w_ref, o_ref, h_ref):
    _proj_fill(x_ref, nw_ref, sc_ref, sh_ref, h_ref)
    o_ref[...] = _dot(h_ref[...], w_ref[...]).astype(o_ref.dtype)


def _qkv_body(x_ref, nw_ref, sc_ref, sh_ref, wq_ref, wk_ref, wv_ref, o_ref, h_ref, *, tiles, k_scale):
    _proj_fill(x_ref, nw_ref, sc_ref, sh_ref, h_ref)
    j = pl.program_id(1)
    q_end, k_end = tiles[0], tiles[0] + tiles[1]

    @pl.when(j < q_end)
    def _():
        o_ref[...] = _dot(h_ref[...], wq_ref[...]).astype(o_ref.dtype)

    @pl.when((j >= q_end) & (j < k_end))
    def _():
        o_ref[...] = (_dot(h_ref[...], wk_ref[...]) * k_scale).astype(o_ref.dtype)

    @pl.when(j >= k_end)
    def _():
        o_ref[...] = _dot(h_ref[...], wv_ref[...]).astype(o_ref.dtype)


def _glu_body(x_ref, nw_ref, sc_ref, sh_ref, wa_ref, wb_ref, ba_ref, bb_ref, o_ref, h_ref):
    _proj_fill(x_ref, nw_ref, sc_ref, sh_ref, h_ref)
    h = h_ref[...]
    for c0 in range(0, o_ref.shape[1], MXU_DIM):
        cols = slice(c0, c0 + MXU_DIM)
        a = _dot(h, wa_ref[:, cols]) + ba_ref[:, cols]
        b = _dot(h, wb_ref[:, cols]) + bb_ref[:, cols]
        o_ref[:, cols] = (a * _sigmoid(b)).astype(o_ref.dtype)


def _headnorm_body(x_ref, nw_ref, sc_ref, sh_ref, w_ref, gain_ref, *rest, n_norm, scale, rope):
    if rope:
        cos_ref, sin_ref, o_ref, h_ref = rest
    else:
        o_ref, h_ref = rest
    _proj_fill(x_ref, nw_ref, sc_ref, sh_ref, h_ref)
    h = h_ref[...]
    hd = ATT_HEAD_DIM
    for c0 in range(0, o_ref.shape[1], MXU_DIM):
        acc = _dot(h, w_ref[:, c0:c0 + MXU_DIM])
        for h0 in range(0, MXU_DIM, hd):
            a = acc[:, h0:h0 + hd]
            if c0 + h0 < n_norm:
                ms = jnp.mean(a * a, axis=-1, keepdims=True)
                a = (a * lax.rsqrt(ms + RMS_EPS)) * gain_ref[...]
                if rope:
                    lane = lax.broadcasted_iota(jnp.int32, a.shape, 1)
                    quarter = hd // 4
                    partner = jnp.where((lane & (2 * quarter - 1)) < quarter,
                                        pltpu.roll(a, hd - quarter, 1), pltpu.roll(a, quarter, 1))
                    a = a * cos_ref[...] + partner * sin_ref[...]
                if scale != 1.0:
                    a = a * scale
            o_ref[:, c0 + h0:c0 + h0 + hd] = a.astype(o_ref.dtype)


def _norm_proj(x, modr, layer, mod_base, nw, w, *, rows_per_sample, out_dtype, mode="plain",
               bias=None, gain=None, rope=None, n_norm=0, scale=1.0):
    rows = x.shape[0]
    tm, tn = PROJ_ROW_TILE, COL_TILE
    if mode == "qkv":
        tiles = [_tiles(p[0].shape[2], tn) for p in w]
        n_out = sum(tiles) * tn
    else:
        n_w = w[0].shape[2]
        n_out = _tiles(n_w, 2) if mode == "glu" else n_w
    w_spec = lambda off: _slot(w, (D_MODEL, tn), lambda i, j: (0, j + off))
    specs = [
        pl.BlockSpec((tm, D_MODEL), _row_spec),
        _slot(nw, (1, D_MODEL), lambda i, j: (0, 0)),
        _mod_spec(layer, mod_base + 1, rows_per_sample, tm),
        _mod_spec(layer, mod_base + 0, rows_per_sample, tm),
    ]
    args = [x, nw[0], modr, modr]
    if mode == "plain":
        body = _proj_body
        specs += [w_spec(0)]
        args += [w[0]]
    elif mode == "qkv":
        body = functools.partial(_qkv_body, tiles=tiles, k_scale=scale)
        start = 0
        for p, n in zip(w, tiles):
            clamp = lambda j, start=start, n=n: jnp.clip(j - start, 0, n - 1)
            specs += [_slot(p, (D_MODEL, tn), lambda i, j, clamp=clamp: (0, clamp(j)))]
            args += [p[0]]
            start += n
    elif mode == "glu":
        body = _glu_body
        off = _tiles(n_out, tn)
        b_spec = lambda o: _slot(bias, (1, tn), lambda i, j: (0, j + o))
        specs += [w_spec(0), w_spec(off), b_spec(0), b_spec(off)]
        args += [w[0], w[0], bias[0], bias[0]]
    else:
        body = functools.partial(_headnorm_body, n_norm=n_norm, scale=scale, rope=rope is not None)
        specs += [w_spec(0), _slot(gain, (1, ATT_HEAD_DIM), lambda i, j: (0, 0))]
        args += [w[0], gain[0]]
        if rope is not None:
            seq_tiles = _tiles(SEQ, tm)
            specs += [pl.BlockSpec((tm, ATT_HEAD_DIM), lambda i, j: (i % seq_tiles, 0))] * 2
            args += list(rope)
    return pl.pallas_call(
        body,
        out_shape=jax.ShapeDtypeStruct((rows, n_out), out_dtype),
        grid=(_tiles(rows, tm), _tiles(n_out, tn)),
        in_specs=specs,
        out_specs=pl.BlockSpec((tm, tn), lambda i, j: (i, j)),
        scratch_shapes=[pltpu.VMEM((tm, D_MODEL), BF16)],
        compiler_params=_params("parallel", "arbitrary"),
        name="norm_proj_" + mode,
    )(*args)


def _resid_body(a_ref, w_ref, *rest, has_bias):
    if has_bias:
        b_ref, x_ref, g_ref, o_ref = rest
    else:
        x_ref, g_ref, o_ref = rest
    acc = _dot(a_ref[...], w_ref[...])
    if has_bias:
        acc = acc + b_ref[...]
    o_ref[...] = x_ref[...] + g_ref[...] * acc


def _matmul_resid(a, w, bias, x, modr, layer, mod_base, *, rows_per_sample):
    rows, k = a.shape
    tm, tn = PROJ_ROW_TILE, COL_TILE
    specs = [pl.BlockSpec((tm, k), _row_spec), _slot(w, (k, tn), lambda i, j: (0, j))]
    args = [a, w[0]]
    if bias is not None:
        specs += [_slot(bias, (1, tn), lambda i, j: (0, j))]
        args += [bias[0]]
    specs += [pl.BlockSpec((tm, tn), lambda i, j: (i, j)),
              _mod_spec(layer, mod_base + 2, rows_per_sample, tm, tn)]
    args += [x, modr]
    return pl.pallas_call(
        functools.partial(_resid_body, has_bias=bias is not None),
        out_shape=jax.ShapeDtypeStruct((rows, D_MODEL), F32),
        grid=(_tiles(rows, tm), _tiles(D_MODEL, tn)),
        in_specs=specs,
        out_specs=pl.BlockSpec((tm, tn), lambda i, j: (i, j)),
        compiler_params=_params("parallel", "parallel"),
        name="matmul_resid",
    )(*args)


def _fused_steps(x_ref, nw_ref, sc_ref, sh_ref, g_ref, o_ref, h_ref, hidden, w_out_ref):
    def product():
        return g_ref[...] * _dot(hidden(), w_out_ref[...])

    @pl.when(pl.program_id(1) == 0)
    def _():
        _fill_h(x_ref, nw_ref, sc_ref, sh_ref, h_ref)
        o_ref[...] = x_ref[...] + product()

    @pl.when(pl.program_id(1) > 0)
    def _():
        o_ref[...] += product()


def _ffn_body(x_ref, nw_ref, sc_ref, sh_ref, g_ref, wg_ref, wu_ref, wd_ref, dw_ref, dwb_ref, halo_ref,
              o_ref, h_ref, *, seq_len):
    tm = x_ref.shape[0]

    def hidden():
        h = h_ref[...]
        gate = _dot(h, wg_ref[...])
        up = _dot(h, wu_ref[...])
        row = lax.broadcasted_iota(jnp.int32, gate.shape, 0)
        pos = (row + pl.program_id(0) * tm) & (seq_len - 1)
        prev = jnp.where(row == 0, halo_ref[0:1, :], pltpu.roll(gate, 1, 0))
        prev = jnp.where(pos == 0, 0.0, prev)
        nxt = jnp.where(row == tm - 1, halo_ref[1:2, :], pltpu.roll(gate, tm - 1, 0))
        nxt = jnp.where(pos == seq_len - 1, 0.0, nxt)
        conv = prev * dw_ref[0:1, :] + gate * dw_ref[1:2, :] + nxt * dw_ref[2:3, :] + dwb_ref[...]
        return (_silu(conv) * up).astype(BF16)

    _fused_steps(x_ref, nw_ref, sc_ref, sh_ref, g_ref, o_ref, h_ref, hidden, wd_ref)


def _gate_body(x_ref, nw_ref, sc_ref, sh_ref, g_ref, wg_ref, y_ref, wo_ref, o_ref, h_ref):
    def hidden():
        return (_silu(_dot(h_ref[...], wg_ref[...])) * y_ref[...].astype(F32)).astype(BF16)

    _fused_steps(x_ref, nw_ref, sc_ref, sh_ref, g_ref, o_ref, h_ref, hidden, wo_ref)


def _halo_body(x_ref, nw_ref, sc_ref, sh_ref, w_ref, o_ref, *, rows_per_sample_block):
    m = x_ref.shape[0]
    if rows_per_sample_block is None:
        sc, sh = sc_ref[BATCH:BATCH + 1, :], sh_ref[BATCH:BATCH + 1, :]
    else:
        blocks = _tiles(m, rows_per_sample_block)
        pick = lambda ref: jnp.concatenate(
            [jnp.broadcast_to(ref[b:b + 1, :], (rows_per_sample_block, D_MODEL)) for b in range(blocks)], axis=0)
        sc, sh = pick(sc_ref), pick(sh_ref)
    h = _modnorm(x_ref[...], nw_ref[...] * (1.0 + sc), sh).astype(BF16)
    o_ref[...] = _dot(h, w_ref[...])


def _ffn_halo(x, mod, layer, nw, wg, *, rows_per_sample):
    rows = x.shape[0]
    tm, tn = FUSED_ROW_TILE, FUSED_COL_TILE
    nt = _tiles(rows, tm)
    t = jnp.arange(nt, dtype=jnp.int32)
    prev = jnp.maximum(t * tm - 1, 0)
    nxt = jnp.minimum((t + 1) * tm, rows - 1)
    idx = jnp.stack([prev, nxt] + [prev] * (SUBLANES - 2), axis=1).reshape(-1)
    xh = x[idx]
    m = nt * SUBLANES
    f = wg[0].shape[2]
    per_sample = None if rows_per_sample is None else SUBLANES * _tiles(rows_per_sample, tm)
    mod_slab = lambda k: pl.BlockSpec((None, SUBLANES, D_MODEL), lambda j: (layer, 0, k))
    out = pl.pallas_call(
        functools.partial(_halo_body, rows_per_sample_block=per_sample),
        out_shape=jax.ShapeDtypeStruct((m, f), F32),
        grid=(_tiles(f, tn),),
        in_specs=[
            pl.BlockSpec((m, D_MODEL), lambda j: (0, 0)),
            _slot(nw, (1, D_MODEL), lambda j: (0, 0)),
            mod_slab(4),
            mod_slab(3),
            _slot(wg, (D_MODEL, tn), lambda j: (0, j)),
        ],
        out_specs=pl.BlockSpec((m, tn), lambda j: (0, j)),
        compiler_params=_params("parallel"),
        name="ffn_halo",
    )(xh, nw[0], mod, mod, wg[0])
    return out.reshape(nt, SUBLANES, f)


def _fused_head(x, modr, layer, mod_base, nw, rows_per_sample):
    tm = FUSED_ROW_TILE
    specs = [
        pl.BlockSpec((tm, D_MODEL), _row_spec),
        _slot(nw, (1, D_MODEL), lambda i, c: (0, 0)),
        _mod_spec(layer, mod_base + 1, rows_per_sample, tm),
        _mod_spec(layer, mod_base + 0, rows_per_sample, tm),
        _mod_spec(layer, mod_base + 2, rows_per_sample, tm),
    ]
    return specs, [x, nw[0], modr, modr, modr]


def _conv_ffn(x, mod, modr, layer, nw, wg, wu, wd, dw, dwb, *, rows_per_sample, seq_len):
    rows = x.shape[0]
    tm, tf = FUSED_ROW_TILE, FUSED_COL_TILE
    f = wg[0].shape[2]
    halo = _ffn_halo(x, mod, layer, nw, wg, rows_per_sample=rows_per_sample)
    specs, args = _fused_head(x, modr, layer, 3, nw, rows_per_sample)
    specs += [
        _slot(wg, (D_MODEL, tf), lambda i, c: (0, c)),
        _slot(wu, (D_MODEL, tf), lambda i, c: (0, c)),
        _slot(wd, (tf, D_MODEL), lambda i, c: (c, 0)),
        _slot(dw, (FFN_CONV_WIDTH, tf), lambda i, c: (0, c)),
        _slot(dwb, (1, tf), lambda i, c: (0, c)),
        pl.BlockSpec((None, SUBLANES, tf), lambda i, c: (i, 0, c)),
    ]
    args += [wg[0], wu[0], wd[0], dw[0], dwb[0], halo]
    return pl.pallas_call(
        functools.partial(_ffn_body, seq_len=seq_len),
        out_shape=jax.ShapeDtypeStruct((rows, D_MODEL), F32),
        grid=(_tiles(rows, tm), _tiles(f, tf)),
        in_specs=specs,
        out_specs=pl.BlockSpec((tm, D_MODEL), _row_spec),
        scratch_shapes=[pltpu.VMEM((tm, D_MODEL), BF16)],
        compiler_params=_params("parallel", "arbitrary"),
        name="conv_ffn",
    )(*args)


def _gated_out(x, modr, layer, nw, wg, y, wo, *, rows_per_sample):
    rows = x.shape[0]
    tm, tf = FUSED_ROW_TILE, GATE_COL_TILE
    f = wg[0].shape[2]
    specs, args = _fused_head(x, modr, layer, 0, nw, rows_per_sample)
    specs += [
        _slot(wg, (D_MODEL, tf), lambda i, c: (0, c)),
        pl.BlockSpec((tm, tf), lambda i, c: (i, c)),
        _slot(wo, (tf, D_MODEL), lambda i, c: (c, 0)),
    ]
    args += [wg[0], y, wo[0]]
    return pl.pallas_call(
        _gate_body,
        out_shape=jax.ShapeDtypeStruct((rows, D_MODEL), F32),
        grid=(_tiles(rows, tm), _tiles(f, tf)),
        in_specs=specs,
        out_specs=pl.BlockSpec((tm, D_MODEL), _row_spec),
        scratch_shapes=[pltpu.VMEM((tm, D_MODEL), BF16)],
        compiler_params=_params("parallel", "arbitrary"),
        name="gated_out",
    )(*args)


def _ret_body(lg_ref, ql_ref, kl_ref, vl_ref, qc_ref, kc_ref, vc_ref, gn_ref, yl_ref, *rest, with_ctx):
    if with_ctx:
        yc_ref, sf_ref, sb_ref, part_ref = rest
    else:
        sf_ref, sb_ref, part_ref = rest
    head = pl.program_id(1)
    lgf = lg_ref[0, head]
    lgb = lg_ref[1, head]
    c = RET_SCAN_CHUNK
    cf = float(c)
    n = lax.broadcasted_iota(jnp.int32, (c, c), 0).astype(F32)
    m = lax.broadcasted_iota(jnp.int32, (c, c), 1).astype(F32)
    diff = n - m
    intra = (jnp.where(diff >= 0, jnp.exp(lgf * jnp.maximum(diff, 0.0)), 0.0)
             + jnp.where(diff <= 0, jnp.exp(lgb * jnp.maximum(-diff, 0.0)), 0.0))
    pv = lax.broadcasted_iota(jnp.int32, (c, RET_V_DIM), 0).astype(F32)
    pk = lax.broadcasted_iota(jnp.int32, (c, RET_QK_DIM), 0).astype(F32)
    q_dec_f = jnp.exp(lgf * (pv + 1.0))
    q_dec_b = jnp.exp(lgb * (cf - pv))
    k_dec_f = jnp.exp(lgf * (cf - 1.0 - pk))
    k_dec_b = jnp.exp(lgb * pk)
    chunk_dec_f = jnp.exp(lgf * jnp.full((1, RET_V_DIM), cf, F32))
    chunk_dec_b = jnp.exp(lgb * jnp.full((1, RET_V_DIM), cf, F32))

    def group_norm(y):
        mu = jnp.mean(y, axis=-1, keepdims=True)
        yc = y - mu
        var = jnp.mean(yc * yc, axis=-1, keepdims=True)
        return (yc * lax.rsqrt(var + LN_EPS)) * gn_ref[...]

    def intra_term(q, k, v):
        return _dot((_dot_nt(q, k) * intra).astype(BF16), v)

    def key_state(k, v, k_dec):
        return _dot_tn((k.astype(F32) * k_dec).astype(BF16), v)

    assert CTX_LEN == c
    qc, kc, vc = qc_ref[...], kc_ref[...], vc_ref[...]
    if with_ctx:
        yc_ref[...] = group_norm(intra_term(qc, kc, vc)).astype(yc_ref.dtype)
    sf_ref[...] = key_state(kc, vc, k_dec_f)
    sb_ref[...] = key_state(kc, vc, k_dec_b)

    n_lat = _tiles(SEQ, c)
    half = _tiles(n_lat, 2)

    def finish(rows, y, second):
        if second:
            yl_ref[rows, :] = group_norm(y + part_ref[rows, :]).astype(yl_ref.dtype)
        else:
            part_ref[rows, :] = y

    def visit(j, second):
        rows_f = pl.ds(pl.multiple_of(j * c, c), c)
        rows_b = pl.ds(pl.multiple_of((n_lat - 1 - j) * c, c), c)
        q, k, v = ql_ref[rows_f, :], kl_ref[rows_f, :], vl_ref[rows_f, :]
        finish(rows_f, intra_term(q, k, v) + _dot(q, sf_ref[...].astype(BF16)) * q_dec_f, second)
        sf_ref[...] = sf_ref[...] * chunk_dec_f + key_state(k, v, k_dec_f)
        q, k, v = ql_ref[rows_b, :], kl_ref[rows_b, :], vl_ref[rows_b, :]
        finish(rows_b, _dot(q, sb_ref[...].astype(BF16)) * q_dec_b, second)
        sb_ref[...] = sb_ref[...] * chunk_dec_b + key_state(k, v, k_dec_b)

    def first_half(j, carry):
        visit(j, False)
        return carry

    def second_half(j, carry):
        visit(j, True)
        return carry

    lax.fori_loop(0, half, first_half, 0)
    lax.fori_loop(half, n_lat, second_half, 0)


def _retention(qkv_l, qkv_c, log_gamma, gn_w, *, with_ctx):
    dk, dv, h = RET_QK_DIM, RET_V_DIM, RET_HEADS
    hv = h * dv
    k_off = h
    v_off = _tiles(2 * h * dk, dv)
    in_specs = [
        pl.BlockSpec((SEQ, dk), lambda b, n, lg: (b, n)),
        pl.BlockSpec((SEQ, dk), lambda b, n, lg: (b, k_off + n)),
        pl.BlockSpec((SEQ, dv), lambda b, n, lg: (b, v_off + n)),
        pl.BlockSpec((CTX_LEN, dk), lambda b, n, lg: (b, n)),
        pl.BlockSpec((CTX_LEN, dk), lambda b, n, lg: (b, k_off + n)),
        pl.BlockSpec((CTX_LEN, dv), lambda b, n, lg: (b, v_off + n)),
        _slot(gn_w, (1, dv), lambda b, n, lg: (0, n)),
    ]
    out_shape = [jax.ShapeDtypeStruct((BATCH * SEQ, hv), BF16)]
    out_specs = [pl.BlockSpec((SEQ, dv), lambda b, n, lg: (b, n))]
    if with_ctx:
        out_shape.append(jax.ShapeDtypeStruct((BATCH * CTX_LEN, hv), BF16))
        out_specs.append(pl.BlockSpec((CTX_LEN, dv), lambda b, n, lg: (b, n)))
    outs = pl.pallas_call(
        functools.partial(_ret_body, with_ctx=with_ctx),
        out_shape=out_shape,
        grid_spec=pltpu.PrefetchScalarGridSpec(
            num_scalar_prefetch=1,
            grid=(BATCH, h),
            in_specs=in_specs,
            out_specs=out_specs,
            scratch_shapes=[pltpu.VMEM((dk, dv), F32), pltpu.VMEM((dk, dv), F32), pltpu.VMEM((SEQ, dv), F32)],
        ),
        compiler_params=_params("parallel", "parallel"),
        name="retention",
    )(log_gamma, qkv_l, qkv_l, qkv_l, qkv_c, qkv_c, qkv_c, gn_w[0])
    return (outs[0], outs[1]) if with_ctx else (outs[0], None)


def _att_body(q_ref, kc_ref, vc_ref, *rest, with_lat):
    if with_lat:
        kl_ref, vl_ref, o_ref, vtc_ref, vtl_ref = rest
    else:
        o_ref, vtc_ref = rest
    hd = ATT_HEAD_DIM
    tq = q_ref.shape[0]
    kvc = ATT_KV_CHUNK

    @pl.when(pl.program_id(2) == 0)
    def _():
        def fill(vt_ref, v_ref, r0, n):
            vt_ref[0:hd, r0:r0 + n] = v_ref[r0:r0 + n, :].astype(F32).T.astype(BF16)
            extra = lax.broadcasted_iota(jnp.int32, (BF16_ROWS, n), 0)
            vt_ref[hd:, r0:r0 + n] = jnp.where(extra == 0, 1.0, 0.0).astype(BF16)

        fill(vtc_ref, vc_ref, 0, CTX_LEN)
        if with_lat:
            for c in range(_tiles(SEQ, kvc)):
                fill(vtl_ref, vl_ref, c * kvc, kvc)

    q_all = jnp.concatenate([q_ref[:, g * hd:(g + 1) * hd] for g in range(ATT_GROUPS)], axis=0)
    chunks = [(kc_ref, vtc_ref, 0, CTX_LEN)]
    if with_lat:
        chunks += [(kl_ref, vtl_ref, c * kvc, kvc) for c in range(_tiles(SEQ, kvc))]

    def scores(chunk):
        k_ref, _, r0, n = chunk
        return _dot_nt(k_ref[r0:r0 + n, :], q_all)

    def weighted_values(chunk, p):
        _, vt_ref, r0, n = chunk
        return _dot(vt_ref[:, r0:r0 + n], p)

    mx = acc = None
    pending = None
    s_next = scores(chunks[0])
    for idx, chunk in enumerate(chunks):
        s = s_next
        if idx + 1 < len(chunks):
            s_next = scores(chunks[idx + 1])
        if pending is not None:
            pv = weighted_values(pending[0], pending[1])
            acc = pv if acc is None else acc * pending[2] + pv
        mx_new = jnp.max(s, axis=0, keepdims=True)
        alpha = None
        if mx is not None:
            mx_new = jnp.maximum(mx, mx_new)
            alpha = jnp.exp2(mx - mx_new)
        p = jnp.exp2(s - mx_new)
        mx = mx_new
        pending = (chunk, p.astype(BF16), alpha)
    pv = weighted_values(pending[0], pending[1])
    acc = pv if acc is None else acc * pending[2] + pv
    out = acc[0:hd, :] / acc[hd:hd + 1, :]
    for g in range(ATT_GROUPS):
        o_ref[:, g * hd:(g + 1) * hd] = out[:, g * tq:(g + 1) * tq].T.astype(o_ref.dtype)


def _attention(q, kv_c, kv_l):
    hd, kvh = ATT_HEAD_DIM, ATT_KV_HEADS
    gw = ATT_GROUPS * hd
    n_q = CTX_LEN if kv_l is None else SEQ
    tq = min(ATT_Q_TILE, n_q)
    qt = _tiles(n_q, tq)
    q_spec = pl.BlockSpec((tq, gw), lambda b, n, t: (b * qt + t, n))
    specs = [q_spec,
             pl.BlockSpec((CTX_LEN, hd), lambda b, n, t: (b, n)),
             pl.BlockSpec((CTX_LEN, hd), lambda b, n, t: (b, kvh + n))]
    args = [q, kv_c, kv_c]
    scratch = [pltpu.VMEM((hd + BF16_ROWS, CTX_LEN), BF16)]
    if kv_l is not None:
        specs += [pl.BlockSpec((SEQ, hd), lambda b, n, t: (b, n)),
                  pl.BlockSpec((SEQ, hd), lambda b, n, t: (b, kvh + n))]
        args += [kv_l, kv_l]
        scratch += [pltpu.VMEM((hd + BF16_ROWS, SEQ), BF16)]
    return pl.pallas_call(
        functools.partial(_att_body, with_lat=kv_l is not None),
        out_shape=jax.ShapeDtypeStruct(q.shape, BF16),
        grid=(BATCH, kvh, qt),
        in_specs=specs,
        out_specs=q_spec,
        scratch_shapes=scratch,
        compiler_params=_params("parallel", "parallel", "arbitrary"),
        name="attention",
    )(*args)


def _dwconv_body(ap_ref, a_ref, an_ref, dw_ref, dwb_ref, lnw_ref, lnb_ref, o_ref, buf_ref, cv_ref, *, seq_tiles):
    tm = a_ref.shape[0]
    pos = pl.program_id(0) % seq_tiles
    buf_ref[0:CONV_HALO, :] = jnp.where(pos == 0, 0.0, ap_ref[...])
    buf_ref[CONV_HALO:CONV_HALO + tm, :] = a_ref[...]
    buf_ref[CONV_HALO + tm:, :] = jnp.where(pos == seq_tiles - 1, 0.0, an_ref[...])
    first = CONV_HALO - CONV_WIDTH // 2
    rows = tm + 2 * CONV_HALO
    for cc in range(_tiles(D_MODEL, CONV_COL_CHUNK)):
        cols = slice(cc * CONV_COL_CHUNK, (cc + 1) * CONV_COL_CHUNK)
        acc = jnp.broadcast_to(dwb_ref[:, cols], (tm, CONV_COL_CHUNK))
        window = buf_ref[:, cols]
        for shift in range(SUBLANES):
            taps = [d for d in range(CONV_WIDTH) if (first + d) % SUBLANES == shift]
            shifted = window if shift == 0 else pltpu.roll(window, rows - shift, 0)
            for d in taps:
                r0 = first + d - shift
                acc = acc + shifted[r0:r0 + tm, :] * dw_ref[d:d + 1, cols]
        cv_ref[:, cols] = acc
    y = cv_ref[...]
    mu = jnp.mean(y, axis=-1, keepdims=True)
    yc = y - mu
    var = jnp.mean(yc * yc, axis=-1, keepdims=True)
    z = (yc * lax.rsqrt(var + LN_EPS)) * lnw_ref[...] + lnb_ref[...]
    o_ref[...] = _silu(z).astype(o_ref.dtype)


def _dwconv_ln_silu(a, dw, dwb, lnw, lnb, *, seq_len):
    rows = a.shape[0]
    tm = CONV_ROW_TILE
    per = _tiles(tm, CONV_HALO)
    last = _tiles(rows, CONV_HALO) - 1
    vec = lambda p: _slot(p, (1, D_MODEL), lambda i: (0, 0))
    return pl.pallas_call(
        functools.partial(_dwconv_body, seq_tiles=_tiles(seq_len, tm)),
        out_shape=jax.ShapeDtypeStruct((rows, D_MODEL), BF16),
        grid=(_tiles(rows, tm),),
        in_specs=[
            pl.BlockSpec((CONV_HALO, D_MODEL), lambda i: (jnp.maximum(i * per - 1, 0), 0)),
            pl.BlockSpec((tm, D_MODEL), lambda i: (i, 0)),
            pl.BlockSpec((CONV_HALO, D_MODEL), lambda i: (jnp.minimum((i + 1) * per, last), 0)),
            _slot(dw, (CONV_WIDTH, D_MODEL), lambda i: (0, 0)),
            vec(dwb), vec(lnw), vec(lnb),
        ],
        out_specs=pl.BlockSpec((tm, D_MODEL), lambda i: (i, 0)),
        scratch_shapes=[pltpu.VMEM((tm + 2 * CONV_HALO, D_MODEL), F32), pltpu.VMEM((tm, D_MODEL), F32)],
        compiler_params=_params("parallel"),
        name="dwconv_ln_silu",
    )(a, a, a, dw[0], dwb[0], lnw[0], lnb[0])


def _rope_tables():
    nf = ATT_HEAD_DIM // 4
    t = jnp.arange(SEQ)
    inv_freq = ROPE_BASE ** (-jnp.arange(nf, dtype=F32) / nf)
    ang_r = (t // GRID_W).astype(F32)[:, None] * inv_freq[None, :]
    ang_c = (t % GRID_W).astype(F32)[:, None] * inv_freq[None, :]
    cos = jnp.concatenate([jnp.cos(ang_r)] * 2 + [jnp.cos(ang_c)] * 2, axis=-1)
    sin = jnp.concatenate([-jnp.sin(ang_r), jnp.sin(ang_r), -jnp.sin(ang_c), jnp.sin(ang_c)], axis=-1)
    return cos, sin


def _rows(v):
    return v.reshape(v.shape[0], 1, v.shape[1])


def kernel(x, c, ctx, c_ctx, mod_w, mod_b, norm1_w, norm2_w, ffn_w_gate, ffn_w_up, ffn_dw, ffn_dw_b, ffn_w_down, ret_wq, ret_wk, ret_wv, ret_wg, ret_wo, ret_decay, ret_gn_w, att_wq, att_wkv, att_q_gain, att_k_gain, att_wo, cnv_w1, cnv_b1, cnv_dw, cnv_dw_b, cnv_ln_w, cnv_ln_b, cnv_w2, cnv_b2):
    xl = x.reshape(BATCH * SEQ, D_MODEL)
    xc = ctx.reshape(BATCH * CTX_LEN, D_MODEL)

    cin = jnp.concatenate([c, c_ctx[None, :], jnp.zeros((SUBLANES - BATCH - 1, D_MODEL), F32)], axis=0)
    mod = _modulation(cin, mod_w, mod_b)
    modr = mod.reshape(DEPTH, SUBLANES, N_MOD, D_MODEL).transpose(0, 2, 1, 3)[:, :, :, None, :]
    rope = _rope_tables()

    bf = lambda w: w.astype(BF16)
    norm1, norm2 = _rows(norm1_w), _rows(norm2_w)
    ffn_wg, ffn_wu, ffn_wd, ffn_b = bf(ffn_w_gate), bf(ffn_w_up), bf(ffn_w_down), _rows(ffn_dw_b)
    ret_q, ret_k, ret_v = bf(ret_wq), bf(ret_wk), bf(ret_wv)
    ret_g, ret_o, ret_gn = bf(ret_wg), bf(ret_wo), _rows(ret_gn_w)
    ret_lg = -jnp.exp(ret_decay.astype(F32))
    att_q, att_kv, att_o = bf(att_wq), bf(att_wkv), bf(att_wo)
    att_qg, att_kg = _rows(att_q_gain), _rows(att_k_gain)
    cnv_1, cnv_2 = bf(cnv_w1), bf(cnv_w2)
    cnv_b1r, cnv_b2r, cnv_dwb, cnv_lnw, cnv_lnb = map(_rows, (cnv_b1, cnv_b2, cnv_dw_b, cnv_ln_w, cnv_ln_b))

    for i in range(DEPTH):
        with_ctx = i < DEPTH - 1
        kind, j = i % N_MIXERS, i // N_MIXERS
        nw1, nw2 = (norm1, i), (norm2, i)
        if kind == 0:
            qkv_args = dict(out_dtype=BF16, mode="qkv", scale=RET_QK_DIM ** -0.5)
            w_qkv = ((ret_q, j), (ret_k, j), (ret_v, j))
            qkv_l = _norm_proj(xl, modr, i, 0, nw1, w_qkv, rows_per_sample=SEQ, **qkv_args)
            qkv_c = _norm_proj(xc, modr, i, 0, nw1, w_qkv, rows_per_sample=None, **qkv_args)
            y_l, y_c = _retention(qkv_l, qkv_c, ret_lg[j], (ret_gn, j), with_ctx=with_ctx)
            xl = _gated_out(xl, modr, i, nw1, (ret_g, j), y_l, (ret_o, j), rows_per_sample=SEQ)
            if with_ctx:
                xc = _gated_out(xc, modr, i, nw1, (ret_g, j), y_c, (ret_o, j), rows_per_sample=None)
        elif kind == 1:
            n_k = ATT_KV_HEADS * ATT_HEAD_DIM
            q_args = dict(out_dtype=BF16, mode="headnorm", gain=(att_qg, j), n_norm=D_MODEL, scale=ATT_Q_SCALE)
            kv_args = dict(out_dtype=BF16, mode="headnorm", gain=(att_kg, j), n_norm=n_k)
            q_l = _norm_proj(xl, modr, i, 0, nw1, (att_q, j), rows_per_sample=SEQ, rope=rope, **q_args)
            kv_l = _norm_proj(xl, modr, i, 0, nw1, (att_kv, j), rows_per_sample=SEQ, rope=rope, **kv_args)
            kv_c = _norm_proj(xc, modr, i, 0, nw1, (att_kv, j), rows_per_sample=None, **kv_args)
            y_l = _attention(q_l, kv_c, kv_l)
            xl = _matmul_resid(y_l, (att_o, j), None, xl, modr, i, 0, rows_per_sample=SEQ)
            if with_ctx:
                q_c = _norm_proj(xc, modr, i, 0, nw1, (att_q, j), rows_per_sample=None, **q_args)
                y_c = _attention(q_c, kv_c, None)
                xc = _matmul_resid(y_c, (att_o, j), None, xc, modr, i, 0, rows_per_sample=None)
        else:
            def conformer(xs, rows_per_sample, seq_len):
                a = _norm_proj(xs, modr, i, 0, nw1, (cnv_1, j), rows_per_sample=rows_per_sample, out_dtype=F32,
                               mode="glu", bias=(cnv_b1r, j))
                a = _dwconv_ln_silu(a, (cnv_dw, j), (cnv_dwb, j), (cnv_lnw, j), (cnv_lnb, j), seq_len=seq_len)
                return _matmul_resid(a, (cnv_2, j), (cnv_b2r, j), xs, modr, i, 0, rows_per_sample=rows_per_sample)

            xl = conformer(xl, SEQ, SEQ)
            if with_ctx:
                xc = conformer(xc, None, CTX_LEN)
        ffn = functools.partial(_conv_ffn, mod=mod, modr=modr, layer=i, nw=nw2, wg=(ffn_wg, i), wu=(ffn_wu, i),
                                wd=(ffn_wd, i), dw=(ffn_dw, i), dwb=(ffn_b, i))
        xl = ffn(xl, rows_per_sample=SEQ, seq_len=SEQ)
        if with_ctx:
            xc = ffn(xc, rows_per_sample=None, seq_len=CTX_LEN)
    return xl.reshape(BATCH, SEQ, D_MODEL)
```

```python
import functools

import jax
import jax.numpy as jnp
from jax import lax
from jax.experimental import pallas as pl
from jax.experimental.pallas import tpu as pltpu

D_MODEL = 2048
BATCH = 4
SEQ = 4096
DEPTH = 4
GRID_W = 64
CTX_LEN = 256
N_MIXERS = 3
RET_HEADS = 8
RET_QK_DIM = D_MODEL // RET_HEADS
RET_V_DIM = 2 * RET_QK_DIM
ATT_HEAD_DIM = 128
ATT_HEADS = D_MODEL // ATT_HEAD_DIM
ATT_KV_HEADS = 4
ATT_GROUPS = ATT_HEADS // ATT_KV_HEADS
ROPE_BASE = 10000.0
CONV_WIDTH = 31
FFN_DIM = ((8 * D_MODEL // 3 + 255) // 256) * 256
FFN_CONV_WIDTH = 3
N_MOD = 6
RMS_EPS = 1e-6
LN_EPS = 1e-5

ATT_Q_SCALE = ATT_HEAD_DIM ** -0.5 * 1.4426950408889634

BF16 = jnp.bfloat16
F32 = jnp.float32

LANES = 128
SUBLANES = 8
BF16_ROWS = 16
MXU_DIM = 256
VMEM_LIMIT_BYTES = 56 * 1024 * 1024

PROJ_ROW_TILE = 1024
FUSED_ROW_TILE = 512
COL_TILE = 1024
FUSED_COL_TILE = 512
GATE_COL_TILE = 1024
ATT_Q_TILE = 512
ATT_KV_CHUNK = 512
RET_SCAN_CHUNK = 256
CONV_ROW_TILE = 128
CONV_HALO = 16
CONV_COL_CHUNK = 256


def _tiles(extent, tile):
    assert extent % tile == 0, (extent, tile)
    return extent // tile


def _dot(a, b):
    return jnp.dot(a, b, preferred_element_type=F32)


def _dot_nt(a, b):
    return lax.dot_general(a, b, (((1,), (1,)), ((), ())), preferred_element_type=F32)


def _dot_tn(a, b):
    return lax.dot_general(a, b, (((0,), (0,)), ((), ())), preferred_element_type=F32)


def _sigmoid(x):
    return 1.0 / (1.0 + jnp.exp(-x))


def _silu(x):
    return x * _sigmoid(x)


def _modnorm(x, scale, shift):
    ms = jnp.mean(x * x, axis=-1, keepdims=True)
    return (x * lax.rsqrt(ms + RMS_EPS)) * scale + shift


def _fill_h(x_ref, nw_ref, sc_ref, sh_ref, h_ref):
    scale = nw_ref[...] * (1.0 + sc_ref[...])
    shift = sh_ref[...]
    for r in range(0, x_ref.shape[0], BF16_ROWS):
        rows = slice(r, r + BF16_ROWS)
        h_ref[rows, :] = _modnorm(x_ref[rows, :], scale, shift).astype(BF16)


def _params(*semantics):
    return pltpu.CompilerParams(dimension_semantics=semantics, vmem_limit_bytes=VMEM_LIMIT_BYTES)


def _slot(param, block, index):
    slot = param[1]
    return pl.BlockSpec((None,) + tuple(block), lambda *g: (slot,) + tuple(index(*g)))


def _mod_spec(layer, k, rows_per_sample, tm, tn=D_MODEL):
    col = (lambda j: 0) if tn == D_MODEL else (lambda j: j)
    if rows_per_sample is None:
        index = lambda i, j: (layer, k, BATCH, 0, col(j))
    else:
        tiles = _tiles(rows_per_sample, tm)
        index = lambda i, j: (layer, k, i // tiles, 0, col(j))
    return pl.BlockSpec((None, None, None, 1, tn), index)


def _row_spec(i, j):
    return (i, 0)


def _mod_body(c_ref, w_ref, b_ref, o_ref):
    c = c_ref[...]
    o_ref[...] = _dot(_silu(c).astype(BF16), w_ref[...].astype(BF16)) + b_ref[...]


def _modulation(cin, mod_w, mod_b):
    n = N_MOD * D_MODEL
    tn = COL_TILE
    return pl.pallas_call(
        _mod_body,
        out_shape=jax.ShapeDtypeStruct((DEPTH, SUBLANES, n), F32),
        grid=(DEPTH, _tiles(n, tn)),
        in_specs=[
            pl.BlockSpec((SUBLANES, D_MODEL), lambda l, j: (0, 0)),
            pl.BlockSpec((None, D_MODEL, tn), lambda l, j: (l, 0, j)),
            pl.BlockSpec((None, 1, tn), lambda l, j: (l, 0, j)),
        ],
        out_specs=pl.BlockSpec((None, SUBLANES, tn), lambda l, j: (l, 0, j)),
        compiler_params=_params("parallel", "parallel"),
        name="modulation",
    )(cin, mod_w, mod_b.reshape(DEPTH, 1, n))


def _proj_steps(x_ref, nw_ref, sc_ref, sh_ref, h_ref, compute):
    @pl.when(pl.program_id(1) == 0)
    def _():
        _fill_h(x_ref, nw_ref, sc_ref, sh_ref, h_ref)
        compute()

    @pl.when(pl.program_id(1) > 0)
    def _():
        compute()


def _proj_body(x_ref, nw_ref, sc_ref, sh_ref, w_ref, o_ref, h_ref):
    def compute():
        o_ref[...] = _dot(h_ref[...], w_ref[...]).astype(o_ref.dtype)

    _proj_steps(x_ref, nw_ref, sc_ref, sh_ref, h_ref, compute)


def _glu_body(x_ref, nw_ref, sc_ref, sh_ref, wa_ref, wb_ref, ba_ref, bb_ref, o_ref, h_ref):
    def compute():
        h = h_ref[...]
        for c0 in range(0, o_ref.shape[1], MXU_DIM):
            cols = slice(c0, c0 + MXU_DIM)
            a = _dot(h, wa_ref[:, cols]) + ba_ref[:, cols]
            b = _dot(h, wb_ref[:, cols]) + bb_ref[:, cols]
            o_ref[:, cols] = (a * _sigmoid(b)).astype(o_ref.dtype)

    _proj_steps(x_ref, nw_ref, sc_ref, sh_ref, h_ref, compute)


def _headnorm_body(x_ref, nw_ref, sc_ref, sh_ref, w_ref, gain_ref, *rest, n_norm, scale, rope):
    if rope:
        cos_ref, sin_ref, o_ref, h_ref = rest
    else:
        o_ref, h_ref = rest
    hd = ATT_HEAD_DIM

    def compute():
        h = h_ref[...]
        for c0 in range(0, o_ref.shape[1], MXU_DIM):
            acc = _dot(h, w_ref[:, c0:c0 + MXU_DIM])
            for h0 in range(0, MXU_DIM, hd):
                a = acc[:, h0:h0 + hd]
                if c0 + h0 < n_norm:
                    ms = jnp.mean(a * a, axis=-1, keepdims=True)
                    a = (a * lax.rsqrt(ms + RMS_EPS)) * gain_ref[...]
                    if rope:
                        lane = lax.broadcasted_iota(jnp.int32, a.shape, 1)
                        quarter = hd // 4
                        partner = jnp.where((lane & (2 * quarter - 1)) < quarter,
                                            pltpu.roll(a, hd - quarter, 1), pltpu.roll(a, quarter, 1))
                        a = a * cos_ref[...] + partner * sin_ref[...]
                    if scale != 1.0:
                        a = a * scale
                o_ref[:, c0 + h0:c0 + h0 + hd] = a.astype(o_ref.dtype)

    _proj_steps(x_ref, nw_ref, sc_ref, sh_ref, h_ref, compute)


def _norm_proj(x, modr, layer, mod_base, nw, w, *, rows_per_sample, out_dtype, mode="plain",
               bias=None, gain=None, rope=None, n_norm=0, scale=1.0):
    rows = x.shape[0]
    tm, tn = PROJ_ROW_TILE, COL_TILE
    n_w = w[0].shape[2]
    n_out = _tiles(n_w, 2) if mode == "glu" else n_w
    w_spec = lambda off: _slot(w, (D_MODEL, tn), lambda i, j: (0, j + off))
    specs = [
        pl.BlockSpec((tm, D_MODEL), _row_spec),
        _slot(nw, (1, D_MODEL), lambda i, j: (0, 0)),
        _mod_spec(layer, mod_base + 1, rows_per_sample, tm),
        _mod_spec(layer, mod_base + 0, rows_per_sample, tm),
    ]
    args = [x, nw[0], modr, modr]
    if mode == "plain":
        body = _proj_body
        specs += [w_spec(0)]
        args += [w[0]]
    elif mode == "glu":
        body = _glu_body
        off = _tiles(n_out, tn)
        b_spec = lambda o: _slot(bias, (1, tn), lambda i, j: (0, j + o))
        specs += [w_spec(0), w_spec(off), b_spec(0), b_spec(off)]
        args += [w[0], w[0], bias[0], bias[0]]
    else:
        body = functools.partial(_headnorm_body, n_norm=n_norm, scale=scale, rope=rope is not None)
        specs += [w_spec(0), _slot(gain, (1, ATT_HEAD_DIM), lambda i, j: (0, 0))]
        args += [w[0], gain[0]]
        if rope is not None:
            seq_tiles = _tiles(SEQ, tm)
            specs += [pl.BlockSpec((tm, ATT_HEAD_DIM), lambda i, j: (i % seq_tiles, 0))] * 2
            args += list(rope)
    return pl.pallas_call(
        body,
        out_shape=jax.ShapeDtypeStruct((rows, n_out), out_dtype),
        grid=(_tiles(rows, tm), _tiles(n_out, tn)),
        in_specs=specs,
        out_specs=pl.BlockSpec((tm, tn), lambda i, j: (i, j)),
        scratch_shapes=[pltpu.VMEM((tm, D_MODEL), BF16)],
        compiler_params=_params("parallel", "arbitrary"),
        name="norm_proj_" + mode,
    )(*args)


def _resid_body(a_ref, w_ref, *rest, has_bias):
    if has_bias:
        b_ref, x_ref, g_ref, o_ref = rest
    else:
        x_ref, g_ref, o_ref = rest
    acc = _dot(a_ref[...], w_ref[...])
    if has_bias:
        acc = acc + b_ref[...]
    o_ref[...] = x_ref[...] + g_ref[...] * acc


def _matmul_resid(a, w, bias, x, modr, layer, mod_base, *, rows_per_sample):
    rows, k = a.shape
    tm, tn = PROJ_ROW_TILE, COL_TILE
    specs = [pl.BlockSpec((tm, k), _row_spec), _slot(w, (k, tn), lambda i, j: (0, j))]
    args = [a, w[0]]
    if bias is not None:
        specs += [_slot(bias, (1, tn), lambda i, j: (0, j))]
        args += [bias[0]]
    specs += [pl.BlockSpec((tm, tn), lambda i, j: (i, j)),
              _mod_spec(layer, mod_base + 2, rows_per_sample, tm, tn)]
    args += [x, modr]
    return pl.pallas_call(
        functools.partial(_resid_body, has_bias=bias is not None),
        out_shape=jax.ShapeDtypeStruct((rows, D_MODEL), F32),
        grid=(_tiles(rows, tm), _tiles(D_MODEL, tn)),
        in_specs=specs,
        out_specs=pl.BlockSpec((tm, tn), lambda i, j: (i, j)),
        compiler_params=_params("parallel", "parallel"),
        name="matmul_resid",
    )(*args)


def _fused_steps(x_ref, nw_ref, sc_ref, sh_ref, g_ref, o_ref, h_ref, hidden, w_out_ref):
    def product():
        return g_ref[...] * _dot(hidden(), w_out_ref[...])

    @pl.when(pl.program_id(1) == 0)
    def _():
        _fill_h(x_ref, nw_ref, sc_ref, sh_ref, h_ref)
        o_ref[...] = x_ref[...] + product()

    @pl.when(pl.program_id(1) > 0)
    def _():
        o_ref[...] += product()


def _ffn_body(x_ref, nw_ref, sc_ref, sh_ref, g_ref, wg_ref, wu_ref, wd_ref, dw_ref, dwb_ref, halo_ref,
              o_ref, h_ref, *, seq_len):
    tm = x_ref.shape[0]

    def hidden():
        h = h_ref[...]
        gate = _dot(h, wg_ref[...])
        up = _dot(h, wu_ref[...])
        prev, nxt = pltpu.roll(gate, 1, 0), pltpu.roll(gate, tm - 1, 0)
        if seq_len % tm == 0:
            tile = pl.program_id(0) % _tiles(seq_len, tm)
            halo_prev = jnp.where(tile == 0, 0.0, halo_ref[0:1, :])
            halo_next = jnp.where(tile == _tiles(seq_len, tm) - 1, 0.0, halo_ref[1:2, :])
            sub = lax.broadcasted_iota(jnp.int32, (SUBLANES, gate.shape[1]), 0)
            prev = jnp.concatenate([jnp.where(sub == 0, halo_prev, prev[:SUBLANES]), prev[SUBLANES:]], axis=0)
            nxt = jnp.concatenate([nxt[:tm - SUBLANES],
                                   jnp.where(sub == SUBLANES - 1, halo_next, nxt[tm - SUBLANES:])], axis=0)
        else:
            row = lax.broadcasted_iota(jnp.int32, gate.shape, 0)
            pos = (row + pl.program_id(0) * tm) & (seq_len - 1)
            prev = jnp.where(pos == 0, 0.0, jnp.where(row == 0, halo_ref[0:1, :], prev))
            nxt = jnp.where(pos == seq_len - 1, 0.0, jnp.where(row == tm - 1, halo_ref[1:2, :], nxt))
        conv = prev * dw_ref[0:1, :] + gate * dw_ref[1:2, :] + nxt * dw_ref[2:3, :] + dwb_ref[...]
        return (_silu(conv) * up).astype(BF16)

    _fused_steps(x_ref, nw_ref, sc_ref, sh_ref, g_ref, o_ref, h_ref, hidden, wd_ref)


def _gate_body(x_ref, nw_ref, sc_ref, sh_ref, g_ref, wg_ref, y_ref, gn_ref, wo_ref, o_ref, h_ref):
    def hidden():
        gate = _silu(_dot(h_ref[...], wg_ref[...]))
        heads = []
        for h0 in range(0, y_ref.shape[1], RET_V_DIM):
            cols = slice(h0, h0 + RET_V_DIM)
            y = y_ref[:, cols].astype(F32)
            yc = y - jnp.mean(y, axis=-1, keepdims=True)
            var = jnp.mean(yc * yc, axis=-1, keepdims=True)
            heads.append((yc * lax.rsqrt(var + LN_EPS)) * gn_ref[:, cols])
        return (gate * jnp.concatenate(heads, axis=1)).astype(BF16)

    _fused_steps(x_ref, nw_ref, sc_ref, sh_ref, g_ref, o_ref, h_ref, hidden, wo_ref)


def _halo_body(x_ref, nw_ref, sc_ref, sh_ref, w_ref, o_ref, *, rows_per_sample_block):
    m = x_ref.shape[0]
    if rows_per_sample_block is None:
        sc, sh = sc_ref[BATCH:BATCH + 1, :], sh_ref[BATCH:BATCH + 1, :]
    else:
        blocks = _tiles(m, rows_per_sample_block)
        pick = lambda ref: jnp.concatenate(
            [jnp.broadcast_to(ref[b:b + 1, :], (rows_per_sample_block, D_MODEL)) for b in range(blocks)], axis=0)
        sc, sh = pick(sc_ref), pick(sh_ref)
    h = _modnorm(x_ref[...], nw_ref[...] * (1.0 + sc), sh).astype(BF16)
    o_ref[...] = _dot(h, w_ref[...])


def _ffn_halo(x, mod, layer, nw, wg, *, rows_per_sample):
    rows = x.shape[0]
    tm, tn = FUSED_ROW_TILE, FUSED_COL_TILE
    nt = _tiles(rows, tm)
    t = jnp.arange(nt, dtype=jnp.int32)
    prev = jnp.maximum(t * tm - 1, 0)
    nxt = jnp.minimum((t + 1) * tm, rows - 1)
    idx = jnp.stack([prev, nxt] + [prev] * (SUBLANES - 2), axis=1).reshape(-1)
    xh = x[idx]
    m = nt * SUBLANES
    f = wg[0].shape[2]
    per_sample = None if rows_per_sample is None else SUBLANES * _tiles(rows_per_sample, tm)
    mod_slab = lambda k: pl.BlockSpec((None, SUBLANES, D_MODEL), lambda j: (layer, 0, k))
    out = pl.pallas_call(
        functools.partial(_halo_body, rows_per_sample_block=per_sample),
        out_shape=jax.ShapeDtypeStruct((m, f), F32),
        grid=(_tiles(f, tn),),
        in_specs=[
            pl.BlockSpec((m, D_MODEL), lambda j: (0, 0)),
            _slot(nw, (1, D_MODEL), lambda j: (0, 0)),
            mod_slab(4),
            mod_slab(3),
            _slot(wg, (D_MODEL, tn), lambda j: (0, j)),
        ],
        out_specs=pl.BlockSpec((m, tn), lambda j: (0, j)),
        compiler_params=_params("parallel"),
        name="ffn_halo",
    )(xh, nw[0], mod, mod, wg[0])
    return out.reshape(nt, SUBLANES, f)


def _fused_head(x, modr, layer, mod_base, nw, rows_per_sample):
    tm = FUSED_ROW_TILE
    specs = [
        pl.BlockSpec((tm, D_MODEL), _row_spec),
        _slot(nw, (1, D_MODEL), lambda i, c: (0, 0)),
        _mod_spec(layer, mod_base + 1, rows_per_sample, tm),
        _mod_spec(layer, mod_base + 0, rows_per_sample, tm),
        _mod_spec(layer, mod_base + 2, rows_per_sample, tm),
    ]
    return specs, [x, nw[0], modr, modr, modr]


def _conv_ffn(x, mod, modr, layer, nw, wg, wu, wd, dw, dwb, *, rows_per_sample, seq_len):
    rows = x.shape[0]
    tm, tf = FUSED_ROW_TILE, FUSED_COL_TILE
    f = wg[0].shape[2]
    halo = _ffn_halo(x, mod, layer, nw, wg, rows_per_sample=rows_per_sample)
    specs, args = _fused_head(x, modr, layer, 3, nw, rows_per_sample)
    specs += [
        _slot(wg, (D_MODEL, tf), lambda i, c: (0, c)),
        _slot(wu, (D_MODEL, tf), lambda i, c: (0, c)),
        _slot(wd, (tf, D_MODEL), lambda i, c: (c, 0)),
        _slot(dw, (FFN_CONV_WIDTH, tf), lambda i, c: (0, c)),
        _slot(dwb, (1, tf), lambda i, c: (0, c)),
        pl.BlockSpec((None, SUBLANES, tf), lambda i, c: (i, 0, c)),
    ]
    args += [wg[0], wu[0], wd[0], dw[0], dwb[0], halo]
    return pl.pallas_call(
        functools.partial(_ffn_body, seq_len=seq_len),
        out_shape=jax.ShapeDtypeStruct((rows, D_MODEL), F32),
        grid=(_tiles(rows, tm), _tiles(f, tf)),
        in_specs=specs,
        out_specs=pl.BlockSpec((tm, D_MODEL), _row_spec),
        scratch_shapes=[pltpu.VMEM((tm, D_MODEL), BF16)],
        compiler_params=_params("parallel", "arbitrary"),
        name="conv_ffn",
    )(*args)


def _gated_out(x, modr, layer, nw, wg, y, gn, wo, *, rows_per_sample):
    rows = x.shape[0]
    tm, tf = FUSED_ROW_TILE, GATE_COL_TILE
    f = wg[0].shape[2]
    specs, args = _fused_head(x, modr, layer, 0, nw, rows_per_sample)
    specs += [
        _slot(wg, (D_MODEL, tf), lambda i, c: (0, c)),
        pl.BlockSpec((tm, tf), lambda i, c: (i, c)),
        _slot(gn, (1, tf), lambda i, c: (0, c)),
        _slot(wo, (tf, D_MODEL), lambda i, c: (c, 0)),
    ]
    args += [wg[0], y, gn[0], wo[0]]
    return pl.pallas_call(
        _gate_body,
        out_shape=jax.ShapeDtypeStruct((rows, D_MODEL), F32),
        grid=(_tiles(rows, tm), _tiles(f, tf)),
        in_specs=specs,
        out_specs=pl.BlockSpec((tm, D_MODEL), _row_spec),
        scratch_shapes=[pltpu.VMEM((tm, D_MODEL), BF16)],
        compiler_params=_params("parallel", "arbitrary"),
        name="gated_out",
    )(*args)


def _ret_body(lg_ref, ql_ref, kl_ref, vl_ref, qc_ref, kc_ref, vc_ref, yl_ref, *rest, with_ctx):
    if with_ctx:
        yc_ref, sf_ref, sb_ref, part_ref = rest
    else:
        sf_ref, sb_ref, part_ref = rest
    head = pl.program_id(1)
    lgf = lg_ref[0, head]
    lgb = lg_ref[1, head]
    c = RET_SCAN_CHUNK
    cf = float(c)
    n = lax.broadcasted_iota(jnp.int32, (c, c), 0).astype(F32)
    m = lax.broadcasted_iota(jnp.int32, (c, c), 1).astype(F32)
    diff = n - m
    intra = (jnp.where(diff >= 0, jnp.exp(lgf * jnp.maximum(diff, 0.0)), 0.0)
             + jnp.where(diff <= 0, jnp.exp(lgb * jnp.maximum(-diff, 0.0)), 0.0))
    pv = lax.broadcasted_iota(jnp.int32, (c, RET_V_DIM), 0).astype(F32)
    pk = lax.broadcasted_iota(jnp.int32, (c, RET_QK_DIM), 0).astype(F32)
    q_dec_f = jnp.exp(lgf * (pv + 1.0))
    q_dec_b = jnp.exp(lgb * (cf - pv))
    k_dec_f = jnp.exp(lgf * (cf - 1.0 - pk))
    k_dec_b = jnp.exp(lgb * pk)
    chunk_dec_f = jnp.exp(lgf * jnp.full((1, RET_V_DIM), cf, F32))
    chunk_dec_b = jnp.exp(lgb * jnp.full((1, RET_V_DIM), cf, F32))

    def intra_term(q, k, v):
        return _dot((_dot_nt(q, k) * intra).astype(BF16), v)

    def key_state(k, v, k_dec):
        return _dot_tn((k.astype(F32) * k_dec).astype(BF16), v)

    assert CTX_LEN == c
    qc, kc, vc = qc_ref[...], kc_ref[...], vc_ref[...]
    if with_ctx:
        yc_ref[...] = intra_term(qc, kc, vc).astype(yc_ref.dtype)
    sf_ref[...] = key_state(kc, vc, k_dec_f)
    sb_ref[...] = key_state(kc, vc, k_dec_b)

    n_lat = _tiles(SEQ, c)
    half = _tiles(n_lat, 2)

    def finish(rows, y, second):
        if second:
            yl_ref[rows, :] = (y + part_ref[rows, :]).astype(yl_ref.dtype)
        else:
            part_ref[rows, :] = y

    def visit(j, second):
        rows_f = pl.ds(pl.multiple_of(j * c, c), c)
        rows_b = pl.ds(pl.multiple_of((n_lat - 1 - j) * c, c), c)
        q, k, v = ql_ref[rows_f, :], kl_ref[rows_f, :], vl_ref[rows_f, :]
        finish(rows_f, intra_term(q, k, v) + _dot(q, sf_ref[...].astype(BF16)) * q_dec_f, second)
        sf_ref[...] = sf_ref[...] * chunk_dec_f + key_state(k, v, k_dec_f)
        q, k, v = ql_ref[rows_b, :], kl_ref[rows_b, :], vl_ref[rows_b, :]
        finish(rows_b, _dot(q, sb_ref[...].astype(BF16)) * q_dec_b, second)
        sb_ref[...] = sb_ref[...] * chunk_dec_b + key_state(k, v, k_dec_b)

    def first_half(j, carry):
        visit(j, False)
        return carry

    def second_half(j, carry):
        visit(j, True)
        return carry

    lax.fori_loop(0, half, first_half, 0)
    lax.fori_loop(half, n_lat, second_half, 0)


def _retention(qkv_l, qkv_c, log_gamma, *, with_ctx):
    dk, dv, h = RET_QK_DIM, RET_V_DIM, RET_HEADS
    hv = h * dv
    k_off = h
    v_off = _tiles(2 * h * dk, dv)
    in_specs = [
        pl.BlockSpec((SEQ, dk), lambda b, n, lg: (b, n)),
        pl.BlockSpec((SEQ, dk), lambda b, n, lg: (b, k_off + n)),
        pl.BlockSpec((SEQ, dv), lambda b, n, lg: (b, v_off + n)),
        pl.BlockSpec((CTX_LEN, dk), lambda b, n, lg: (b, n)),
        pl.BlockSpec((CTX_LEN, dk), lambda b, n, lg: (b, k_off + n)),
        pl.BlockSpec((CTX_LEN, dv), lambda b, n, lg: (b, v_off + n)),
    ]
    out_shape = [jax.ShapeDtypeStruct((BATCH * SEQ, hv), BF16)]
    out_specs = [pl.BlockSpec((SEQ, dv), lambda b, n, lg: (b, n))]
    if with_ctx:
        out_shape.append(jax.ShapeDtypeStruct((BATCH * CTX_LEN, hv), BF16))
        out_specs.append(pl.BlockSpec((CTX_LEN, dv), lambda b, n, lg: (b, n)))
    outs = pl.pallas_call(
        functools.partial(_ret_body, with_ctx=with_ctx),
        out_shape=out_shape,
        grid_spec=pltpu.PrefetchScalarGridSpec(
            num_scalar_prefetch=1,
            grid=(BATCH, h),
            in_specs=in_specs,
            out_specs=out_specs,
            scratch_shapes=[pltpu.VMEM((dk, dv), F32), pltpu.VMEM((dk, dv), F32), pltpu.VMEM((SEQ, dv), F32)],
        ),
        compiler_params=_params("parallel", "parallel"),
        name="retention",
    )(log_gamma, qkv_l, qkv_l, qkv_l, qkv_c, qkv_c, qkv_c)
    return (outs[0], outs[1]) if with_ctx else (outs[0], None)


def _att_body(q_ref, kc_ref, vc_ref, *rest, with_lat):
    if with_lat:
        kl_ref, vl_ref, o_ref, vtc_ref, vtl_ref = rest
    else:
        o_ref, vtc_ref = rest
    hd = ATT_HEAD_DIM
    tq = q_ref.shape[0]
    kvc = ATT_KV_CHUNK

    @pl.when(pl.program_id(2) == 0)
    def _():
        def fill(vt_ref, v_ref, r0, n):
            vt_ref[0:hd, r0:r0 + n] = v_ref[r0:r0 + n, :].astype(F32).T.astype(BF16)
            extra = lax.broadcasted_iota(jnp.int32, (BF16_ROWS, n), 0)
            vt_ref[hd:, r0:r0 + n] = jnp.where(extra == 0, 1.0, 0.0).astype(BF16)

        fill(vtc_ref, vc_ref, 0, CTX_LEN)
        if with_lat:
            for c in range(_tiles(SEQ, kvc)):
                fill(vtl_ref, vl_ref, c * kvc, kvc)

    q_all = jnp.concatenate([q_ref[:, g * hd:(g + 1) * hd] for g in range(ATT_GROUPS)], axis=0)
    chunks = [(kc_ref, vtc_ref, 0, CTX_LEN)]
    if with_lat:
        chunks += [(kl_ref, vtl_ref, c * kvc, kvc) for c in range(_tiles(SEQ, kvc))]

    def scores(chunk):
        k_ref, _, r0, n = chunk
        return _dot_nt(k_ref[r0:r0 + n, :], q_all)

    def weighted_values(chunk, p):
        _, vt_ref, r0, n = chunk
        return _dot(vt_ref[:, r0:r0 + n], p)

    mx = acc = None
    pending = None
    s_next = scores(chunks[0])
    for idx, chunk in enumerate(chunks):
        s = s_next
        if idx + 1 < len(chunks):
            s_next = scores(chunks[idx + 1])
        if pending is not None:
            pv = weighted_values(pending[0], pending[1])
            acc = pv if acc is None else acc * pending[2] + pv
        mx_new = jnp.max(s, axis=0, keepdims=True)
        alpha = None
        if mx is not None:
            mx_new = jnp.maximum(mx, mx_new)
            alpha = jnp.exp2(mx - mx_new)
        p = jnp.exp2(s - mx_new)
        mx = mx_new
        pending = (chunk, p.astype(BF16), alpha)
    pv = weighted_values(pending[0], pending[1])
    acc = pv if acc is None else acc * pending[2] + pv
    out = acc[0:hd, :] / acc[hd:hd + 1, :]
    for g in range(ATT_GROUPS):
        o_ref[:, g * hd:(g + 1) * hd] = out[:, g * tq:(g + 1) * tq].T.astype(o_ref.dtype)


def _attention(q, kv_c, kv_l):
    hd, kvh = ATT_HEAD_DIM, ATT_KV_HEADS
    gw = ATT_GROUPS * hd
    n_q = CTX_LEN if kv_l is None else SEQ
    tq = min(ATT_Q_TILE, n_q)
    qt = _tiles(n_q, tq)
    q_spec = pl.BlockSpec((tq, gw), lambda b, n, t: (b * qt + t, n))
    specs = [q_spec,
             pl.BlockSpec((CTX_LEN, hd), lambda b, n, t: (b, n)),
             pl.BlockSpec((CTX_LEN, hd), lambda b, n, t: (b, kvh + n))]
    args = [q, kv_c, kv_c]
    scratch = [pltpu.VMEM((hd + BF16_ROWS, CTX_LEN), BF16)]
    if kv_l is not None:
        specs += [pl.BlockSpec((SEQ, hd), lambda b, n, t: (b, n)),
                  pl.BlockSpec((SEQ, hd), lambda b, n, t: (b, kvh + n))]
        args += [kv_l, kv_l]
        scratch += [pltpu.VMEM((hd + BF16_ROWS, SEQ), BF16)]
    return pl.pallas_call(
        functools.partial(_att_body, with_lat=kv_l is not None),
        out_shape=jax.ShapeDtypeStruct(q.shape, BF16),
        grid=(BATCH, kvh, qt),
        in_specs=specs,
        out_specs=q_spec,
        scratch_shapes=scratch,
        compiler_params=_params("parallel", "parallel", "arbitrary"),
        name="attention",
    )(*args)


def _dwconv_body(ap_ref, a_ref, an_ref, dw_ref, dwb_ref, lnw_ref, lnb_ref, o_ref, buf_ref, cv_ref, *, seq_tiles):
    tm = a_ref.shape[0]
    pos = pl.program_id(0) % seq_tiles
    buf_ref[0:CONV_HALO, :] = jnp.where(pos == 0, 0.0, ap_ref[...])
    buf_ref[CONV_HALO:CONV_HALO + tm, :] = a_ref[...]
    buf_ref[CONV_HALO + tm:, :] = jnp.where(pos == seq_tiles - 1, 0.0, an_ref[...])
    first = CONV_HALO - CONV_WIDTH // 2
    rows = tm + 2 * CONV_HALO
    for cc in range(_tiles(D_MODEL, CONV_COL_CHUNK)):
        cols = slice(cc * CONV_COL_CHUNK, (cc + 1) * CONV_COL_CHUNK)
        acc = jnp.broadcast_to(dwb_ref[:, cols], (tm, CONV_COL_CHUNK))
        window = buf_ref[:, cols]
        for shift in range(SUBLANES):
            taps = [d for d in range(CONV_WIDTH) if (first + d) % SUBLANES == shift]
            shifted = window if shift == 0 else pltpu.roll(window, rows - shift, 0)
            for d in taps:
                r0 = first + d - shift
                acc = acc + shifted[r0:r0 + tm, :] * dw_ref[d:d + 1, cols]
        cv_ref[:, cols] = acc
    y = cv_ref[...]
    mu = jnp.mean(y, axis=-1, keepdims=True)
    yc = y - mu
    var = jnp.mean(yc * yc, axis=-1, keepdims=True)
    z = (yc * lax.rsqrt(var + LN_EPS)) * lnw_ref[...] + lnb_ref[...]
    o_ref[...] = _silu(z).astype(o_ref.dtype)


def _dwconv_ln_silu(a, dw, dwb, lnw, lnb, *, seq_len):
    rows = a.shape[0]
    tm = CONV_ROW_TILE
    per = _tiles(tm, CONV_HALO)
    last = _tiles(rows, CONV_HALO) - 1
    vec = lambda p: _slot(p, (1, D_MODEL), lambda i: (0, 0))
    return pl.pallas_call(
        functools.partial(_dwconv_body, seq_tiles=_tiles(seq_len, tm)),
        out_shape=jax.ShapeDtypeStruct((rows, D_MODEL), BF16),
        grid=(_tiles(rows, tm),),
        in_specs=[
            pl.BlockSpec((CONV_HALO, D_MODEL), lambda i: (jnp.maximum(i * per - 1, 0), 0)),
            pl.BlockSpec((tm, D_MODEL), lambda i: (i, 0)),
            pl.BlockSpec((CONV_HALO, D_MODEL), lambda i: (jnp.minimum((i + 1) * per, last), 0)),
            _slot(dw, (CONV_WIDTH, D_MODEL), lambda i: (0, 0)),
            vec(dwb), vec(lnw), vec(lnb),
        ],
        out_specs=pl.BlockSpec((tm, D_MODEL), lambda i: (i, 0)),
        scratch_shapes=[pltpu.VMEM((tm + 2 * CONV_HALO, D_MODEL), F32), pltpu.VMEM((tm, D_MODEL), F32)],
        compiler_params=_params("parallel"),
        name="dwconv_ln_silu",
    )(a, a, a, dw[0], dwb[0], lnw[0], lnb[0])


def _rope_tables():
    nf = ATT_HEAD_DIM // 4
    t = jnp.arange(SEQ)
    inv_freq = ROPE_BASE ** (-jnp.arange(nf, dtype=F32) / nf)
    ang_r = (t // GRID_W).astype(F32)[:, None] * inv_freq[None, :]
    ang_c = (t % GRID_W).astype(F32)[:, None] * inv_freq[None, :]
    cos = jnp.concatenate([jnp.cos(ang_r)] * 2 + [jnp.cos(ang_c)] * 2, axis=-1)
    sin = jnp.concatenate([-jnp.sin(ang_r), jnp.sin(ang_r), -jnp.sin(ang_c), jnp.sin(ang_c)], axis=-1)
    return cos, sin


def _rows(v):
    return v.reshape(v.shape[0], 1, v.shape[1])


def kernel(x, c, ctx, c_ctx, mod_w, mod_b, norm1_w, norm2_w, ffn_w_gate, ffn_w_up, ffn_dw, ffn_dw_b, ffn_w_down, ret_wq, ret_wk, ret_wv, ret_wg, ret_wo, ret_decay, ret_gn_w, att_wq, att_wkv, att_q_gain, att_k_gain, att_wo, cnv_w1, cnv_b1, cnv_dw, cnv_dw_b, cnv_ln_w, cnv_ln_b, cnv_w2, cnv_b2):
    xl = x.reshape(BATCH * SEQ, D_MODEL)
    xc = ctx.reshape(BATCH * CTX_LEN, D_MODEL)

    cin = jnp.concatenate([c, c_ctx[None, :], jnp.zeros((SUBLANES - BATCH - 1, D_MODEL), F32)], axis=0)
    mod = _modulation(cin, mod_w, mod_b)
    modr = mod.reshape(DEPTH, SUBLANES, N_MOD, D_MODEL).transpose(0, 2, 1, 3)[:, :, :, None, :]
    rope = _rope_tables()

    bf = lambda w: w.astype(BF16)
    norm1, norm2 = _rows(norm1_w), _rows(norm2_w)
    ffn_wg, ffn_wu, ffn_wd, ffn_b = bf(ffn_w_gate), bf(ffn_w_up), bf(ffn_w_down), _rows(ffn_dw_b)
    ret_qkv = bf(jnp.concatenate([ret_wq, ret_wk * (RET_QK_DIM ** -0.5), ret_wv], axis=-1))
    ret_g, ret_o, ret_gn = bf(ret_wg), bf(ret_wo), _rows(ret_gn_w)
    ret_lg = -jnp.exp(ret_decay.astype(F32))
    att_q, att_kv, att_o = bf(att_wq), bf(att_wkv), bf(att_wo)
    att_qg, att_kg = _rows(att_q_gain), _rows(att_k_gain)
    cnv_1, cnv_2 = bf(cnv_w1), bf(cnv_w2)
    cnv_b1r, cnv_b2r, cnv_dwb, cnv_lnw, cnv_lnb = map(_rows, (cnv_b1, cnv_b2, cnv_dw_b, cnv_ln_w, cnv_ln_b))

    for i in range(DEPTH):
        with_ctx = i < DEPTH - 1
        kind, j = i % N_MIXERS, i // N_MIXERS
        nw1, nw2 = (norm1, i), (norm2, i)
        if kind == 0:
            qkv_l = _norm_proj(xl, modr, i, 0, nw1, (ret_qkv, j), rows_per_sample=SEQ, out_dtype=BF16)
            qkv_c = _norm_proj(xc, modr, i, 0, nw1, (ret_qkv, j), rows_per_sample=None, out_dtype=BF16)
            y_l, y_c = _retention(qkv_l, qkv_c, ret_lg[j], with_ctx=with_ctx)
            xl = _gated_out(xl, modr, i, nw1, (ret_g, j), y_l, (ret_gn, j), (ret_o, j), rows_per_sample=SEQ)
            if with_ctx:
                xc = _gated_out(xc, modr, i, nw1, (ret_g, j), y_c, (ret_gn, j), (ret_o, j), rows_per_sample=None)
        elif kind == 1:
            n_k = ATT_KV_HEADS * ATT_HEAD_DIM
            q_args = dict(out_dtype=BF16, mode="headnorm", gain=(att_qg, j), n_norm=D_MODEL, scale=ATT_Q_SCALE)
            kv_args = dict(out_dtype=BF16, mode="headnorm", gain=(att_kg, j), n_norm=n_k)
            q_l = _norm_proj(xl, modr, i, 0, nw1, (att_q, j), rows_per_sample=SEQ, rope=rope, **q_args)
            kv_l = _norm_proj(xl, modr, i, 0, nw1, (att_kv, j), rows_per_sample=SEQ, rope=rope, **kv_args)
            kv_c = _norm_proj(xc, modr, i, 0, nw1, (att_kv, j), rows_per_sample=None, **kv_args)
            y_l = _attention(q_l, kv_c, kv_l)
            xl = _matmul_resid(y_l, (att_o, j), None, xl, modr, i, 0, rows_per_sample=SEQ)
            if with_ctx:
                q_c = _norm_proj(xc, modr, i, 0, nw1, (att_q, j), rows_per_sample=None, **q_args)
                y_c = _attention(q_c, kv_c, None)
                xc = _matmul_resid(y_c, (att_o, j), None, xc, modr, i, 0, rows_per_sample=None)
        else:
            def conformer(xs, rows_per_sample, seq_len):
                a = _norm_proj(xs, modr, i, 0, nw1, (cnv_1, j), rows_per_sample=rows_per_sample, out_dtype=F32,
                               mode="glu", bias=(cnv_b1r, j))
                a = _dwconv_ln_silu(a, (cnv_dw, j), (cnv_dwb, j), (cnv_lnw, j), (cnv_lnb, j), seq_len=seq_len)
                return _matmul_resid(a, (cnv_2, j), (cnv_b2r, j), xs, modr, i, 0, rows_per_sample=rows_per_sample)

            xl = conformer(xl, SEQ, SEQ)
            if with_ctx:
                xc = conformer(xc, None, CTX_LEN)
        ffn = functools.partial(_conv_ffn, mod=mod, modr=modr, layer=i, nw=nw2, wg=(ffn_wg, i), wu=(ffn_wu, i),
                                wd=(ffn_wd, i), dw=(ffn_dw, i), dwb=(ffn_b, i))
        xl = ffn(xl, rows_per_sample=SEQ, seq_len=SEQ)
        if with_ctx:
            xc = ffn(xc, rows_per_sample=None, seq_len=CTX_LEN)
    return xl.reshape(BATCH, SEQ, D_MODEL)
```

```python
import functools

import jax
import jax.numpy as jnp
from jax import lax
from jax.experimental import pallas as pl
from jax.experimental.pallas import tpu as pltpu

D_MODEL = 2048
BATCH = 4
SEQ = 4096
DEPTH = 4
GRID_W = 64
CTX_LEN = 256
N_MIXERS = 3
RET_HEADS = 8
RET_QK_DIM = D_MODEL // RET_HEADS
RET_V_DIM = 2 * RET_QK_DIM
ATT_HEAD_DIM = 128
ATT_HEADS = D_MODEL // ATT_HEAD_DIM
ATT_KV_HEADS = 4
ATT_GROUPS = ATT_HEADS // ATT_KV_HEADS
ROPE_BASE = 10000.0
CONV_WIDTH = 31
FFN_DIM = ((8 * D_MODEL // 3 + 255) // 256) * 256
FFN_CONV_WIDTH = 3
N_MOD = 6
RMS_EPS = 1e-6
LN_EPS = 1e-5

ATT_Q_SCALE = ATT_HEAD_DIM ** -0.5 * 1.4426950408889634

BF16 = jnp.bfloat16
F32 = jnp.float32

LANES = 128
SUBLANES = 8
BF16_ROWS = 16
MXU_DIM = 256
VMEM_LIMIT_BYTES = 56 * 1024 * 1024

PROJ_ROW_TILE = 1024
FUSED_ROW_TILE = 512
COL_TILE = 1024
FUSED_COL_TILE = 512
GATE_COL_TILE = 1024
ATT_Q_TILE = 512
ATT_KV_CHUNK = 512
RET_SCAN_CHUNK = 256
CONV_ROW_TILE = 128
CONV_HALO = 16
CONV_COL_CHUNK = 256


def _tiles(extent, tile):
    assert extent % tile == 0, (extent, tile)
    return extent // tile


def _dot(a, b):
    return jnp.dot(a, b, preferred_element_type=F32)


def _dot_nt(a, b):
    return lax.dot_general(a, b, (((1,), (1,)), ((), ())), preferred_element_type=F32)


def _dot_tn(a, b):
    return lax.dot_general(a, b, (((0,), (0,)), ((), ())), preferred_element_type=F32)


def _sigmoid(x):
    return 1.0 / (1.0 + jnp.exp(-x))


def _silu(x):
    return x * _sigmoid(x)


def _modnorm(x, scale, shift):
    ms = jnp.mean(x * x, axis=-1, keepdims=True)
    return (x * lax.rsqrt(ms + RMS_EPS)) * scale + shift


def _fill_h(x_ref, nw_ref, sc_ref, sh_ref, h_ref):
    scale = nw_ref[...] * (1.0 + sc_ref[...])
    shift = sh_ref[...]
    for r in range(0, x_ref.shape[0], BF16_ROWS):
        rows = slice(r, r + BF16_ROWS)
        h_ref[rows, :] = _modnorm(x_ref[rows, :], scale, shift).astype(BF16)


def _params(*semantics):
    return pltpu.CompilerParams(dimension_semantics=semantics, vmem_limit_bytes=VMEM_LIMIT_BYTES)


def _slot(param, block, index):
    slot = param[1]
    return pl.BlockSpec((None,) + tuple(block), lambda *g: (slot,) + tuple(index(*g)))


def _mod_spec(layer, k, rows_per_sample, tm, tn=D_MODEL):
    col = (lambda j: 0) if tn == D_MODEL else (lambda j: j)
    if rows_per_sample is None:
        index = lambda i, j: (layer, k, BATCH, 0, col(j))
    else:
        tiles = _tiles(rows_per_sample, tm)
        index = lambda i, j: (layer, k, i // tiles, 0, col(j))
    return pl.BlockSpec((None, None, None, 1, tn), index)


def _row_spec(i, j):
    return (i, 0)


def _mod_body(c_ref, w_ref, b_ref, o_ref):
    c = c_ref[...]
    o_ref[...] = _dot(_silu(c).astype(BF16), w_ref[...].astype(BF16)) + b_ref[...]


def _modulation(cin, mod_w, mod_b):
    n = N_MOD * D_MODEL
    tn = COL_TILE
    return pl.pallas_call(
        _mod_body,
        out_shape=jax.ShapeDtypeStruct((DEPTH, SUBLANES, n), F32),
        grid=(DEPTH, _tiles(n, tn)),
        in_specs=[
            pl.BlockSpec((SUBLANES, D_MODEL), lambda l, j: (0, 0)),
            pl.BlockSpec((None, D_MODEL, tn), lambda l, j: (l, 0, j)),
            pl.BlockSpec((None, 1, tn), lambda l, j: (l, 0, j)),
        ],
        out_specs=pl.BlockSpec((None, SUBLANES, tn), lambda l, j: (l, 0, j)),
        compiler_params=_params("parallel", "parallel"),
        name="modulation",
    )(cin, mod_w, mod_b.reshape(DEPTH, 1, n))


def _proj_steps(x_ref, nw_ref, sc_ref, sh_ref, h_ref, compute):
    @pl.when(pl.program_id(1) == 0)
    def _():
        _fill_h(x_ref, nw_ref, sc_ref, sh_ref, h_ref)
        compute()

    @pl.when(pl.program_id(1) > 0)
    def _():
        compute()


def _proj_body(x_ref, nw_ref, sc_ref, sh_ref, w_ref, o_ref, h_ref):
    def compute():
        o_ref[...] = _dot(h_ref[...], w_ref[...]).astype(o_ref.dtype)

    _proj_steps(x_ref, nw_ref, sc_ref, sh_ref, h_ref, compute)


def _glu_body(x_ref, nw_ref, sc_ref, sh_ref, wa_ref, wb_ref, ba_ref, bb_ref, o_ref, h_ref):
    def compute():
        h = h_ref[...]
        for c0 in range(0, o_ref.shape[1], MXU_DIM):
            cols = slice(c0, c0 + MXU_DIM)
            a = _dot(h, wa_ref[:, cols]) + ba_ref[:, cols]
            b = _dot(h, wb_ref[:, cols]) + bb_ref[:, cols]
            o_ref[:, cols] = (a * _sigmoid(b)).astype(o_ref.dtype)

    _proj_steps(x_ref, nw_ref, sc_ref, sh_ref, h_ref, compute)


def _headnorm_body(x_ref, nw_ref, sc_ref, sh_ref, w_ref, *rest, n_norm, scale, rope):
    if rope:
        wp_ref, t1_ref, t2_ref, o_ref, h_ref = rest
    else:
        gain_ref, o_ref, h_ref = rest
    hd = ATT_HEAD_DIM

    def compute():
        h = h_ref[...]
        for c0 in range(0, o_ref.shape[1], MXU_DIM):
            cols = slice(c0, c0 + MXU_DIM)
            acc = _dot(h, w_ref[:, cols])
            swapped = _dot(h, wp_ref[:, cols]) if rope and c0 < n_norm else None
            for h0 in range(0, MXU_DIM, hd):
                a = acc[:, h0:h0 + hd]
                if c0 + h0 < n_norm:
                    inv = lax.rsqrt(jnp.mean(a * a, axis=-1, keepdims=True) + RMS_EPS)
                    if rope:
                        a = (a * t1_ref[...] + swapped[:, h0:h0 + hd] * t2_ref[...]) * inv
                    else:
                        a = (a * inv) * gain_ref[...]
                        if scale != 1.0:
                            a = a * scale
                o_ref[:, c0 + h0:c0 + h0 + hd] = a.astype(o_ref.dtype)

    _proj_steps(x_ref, nw_ref, sc_ref, sh_ref, h_ref, compute)


def _norm_proj(x, modr, layer, mod_base, nw, w, *, rows_per_sample, out_dtype, mode="plain",
               bias=None, gain=None, rope=None, n_norm=0, scale=1.0):
    rows = x.shape[0]
    tm, tn = PROJ_ROW_TILE, COL_TILE
    n_w = w[0].shape[2]
    n_out = _tiles(n_w, 2) if mode == "glu" else n_w
    w_spec = lambda off: _slot(w, (D_MODEL, tn), lambda i, j: (0, j + off))
    specs = [
        pl.BlockSpec((tm, D_MODEL), _row_spec),
        _slot(nw, (1, D_MODEL), lambda i, j: (0, 0)),
        _mod_spec(layer, mod_base + 1, rows_per_sample, tm),
        _mod_spec(layer, mod_base + 0, rows_per_sample, tm),
    ]
    args = [x, nw[0], modr, modr]
    if mode == "plain":
        body = _proj_body
        specs += [w_spec(0)]
        args += [w[0]]
    elif mode == "glu":
        body = _glu_body
        off = _tiles(n_out, tn)
        b_spec = lambda o: _slot(bias, (1, tn), lambda i, j: (0, j + o))
        specs += [w_spec(0), w_spec(off), b_spec(0), b_spec(off)]
        args += [w[0], w[0], bias[0], bias[0]]
    else:
        body = functools.partial(_headnorm_body, n_norm=n_norm, scale=scale, rope=rope is not None)
        specs += [w_spec(0)]
        args += [w[0]]
        if rope is not None:
            w_swapped, t1, t2 = rope
            seq_tiles = _tiles(SEQ, tm)
            specs += [_slot(w_swapped, (D_MODEL, min(tn, n_norm)), lambda i, j: (0, j))]
            specs += [pl.BlockSpec((tm, ATT_HEAD_DIM), lambda i, j: (i % seq_tiles, 0))] * 2
            args += [w_swapped[0], t1, t2]
        else:
            specs += [_slot(gain, (1, ATT_HEAD_DIM), lambda i, j: (0, 0))]
            args += [gain[0]]
    return pl.pallas_call(
        body,
        out_shape=jax.ShapeDtypeStruct((rows, n_out), out_dtype),
        grid=(_tiles(rows, tm), _tiles(n_out, tn)),
        in_specs=specs,
        out_specs=pl.BlockSpec((tm, tn), lambda i, j: (i, j)),
        scratch_shapes=[pltpu.VMEM((tm, D_MODEL), BF16)],
        compiler_params=_params("parallel", "arbitrary"),
        name="norm_proj_" + mode,
    )(*args)


def _resid_body(a_ref, w_ref, *rest, has_bias):
    if has_bias:
        b_ref, x_ref, g_ref, o_ref = rest
    else:
        x_ref, g_ref, o_ref = rest
    acc = _dot(a_ref[...], w_ref[...])
    if has_bias:
        acc = acc + b_ref[...]
    o_ref[...] = x_ref[...] + g_ref[...] * acc


def _matmul_resid(a, w, bias, x, modr, layer, mod_base, *, rows_per_sample):
    rows, k = a.shape
    tm, tn = FUSED_ROW_TILE, D_MODEL
    specs = [pl.BlockSpec((tm, k), _row_spec), _slot(w, (k, tn), lambda i, j: (0, j))]
    args = [a, w[0]]
    if bias is not None:
        specs += [_slot(bias, (1, tn), lambda i, j: (0, j))]
        args += [bias[0]]
    specs += [pl.BlockSpec((tm, tn), lambda i, j: (i, j)),
              _mod_spec(layer, mod_base + 2, rows_per_sample, tm, tn)]
    args += [x, modr]
    return pl.pallas_call(
        functools.partial(_resid_body, has_bias=bias is not None),
        out_shape=jax.ShapeDtypeStruct((rows, D_MODEL), F32),
        grid=(_tiles(rows, tm), _tiles(D_MODEL, tn)),
        in_specs=specs,
        out_specs=pl.BlockSpec((tm, tn), lambda i, j: (i, j)),
        compiler_params=_params("parallel", "parallel"),
        name="matmul_resid",
    )(*args)


def _fused_steps(x_ref, nw_ref, sc_ref, sh_ref, g_ref, o_ref, h_ref, hidden, w_out_ref):
    def product():
        return g_ref[...] * _dot(hidden(), w_out_ref[...])

    @pl.when(pl.program_id(1) == 0)
    def _():
        _fill_h(x_ref, nw_ref, sc_ref, sh_ref, h_ref)
        o_ref[...] = x_ref[...] + product()

    @pl.when(pl.program_id(1) > 0)
    def _():
        o_ref[...] += product()


def _ffn_body(x_ref, nw_ref, sc_ref, sh_ref, g_ref, wg_ref, wu_ref, wd_ref, dw_ref, dwb_ref, halo_ref,
              o_ref, h_ref, *, seq_len):
    tm = x_ref.shape[0]

    def hidden():
        h = h_ref[...]
        gate = _dot(h, wg_ref[...])
        up = _dot(h, wu_ref[...])
        prev, nxt = pltpu.roll(gate, 1, 0), pltpu.roll(gate, tm - 1, 0)
        if seq_len % tm == 0:
            tile = pl.program_id(0) % _tiles(seq_len, tm)
            halo_prev = jnp.where(tile == 0, 0.0, halo_ref[0:1, :])
            halo_next = jnp.where(tile == _tiles(seq_len, tm) - 1, 0.0, halo_ref[1:2, :])
            sub = lax.broadcasted_iota(jnp.int32, (SUBLANES, gate.shape[1]), 0)
            prev = jnp.concatenate([jnp.where(sub == 0, halo_prev, prev[:SUBLANES]), prev[SUBLANES:]], axis=0)
            nxt = jnp.concatenate([nxt[:tm - SUBLANES],
                                   jnp.where(sub == SUBLANES - 1, halo_next, nxt[tm - SUBLANES:])], axis=0)
        else:
            row = lax.broadcasted_iota(jnp.int32, gate.shape, 0)
            pos = (row + pl.program_id(0) * tm) & (seq_len - 1)
            prev = jnp.where(pos == 0, 0.0, jnp.where(row == 0, halo_ref[0:1, :], prev))
            nxt = jnp.where(pos == seq_len - 1, 0.0, jnp.where(row == tm - 1, halo_ref[1:2, :], nxt))
        conv = prev * dw_ref[0:1, :] + gate * dw_ref[1:2, :] + nxt * dw_ref[2:3, :] + dwb_ref[...]
        return (_silu(conv) * up).astype(BF16)

    _fused_steps(x_ref, nw_ref, sc_ref, sh_ref, g_ref, o_ref, h_ref, hidden, wd_ref)


def _gate_body(x_ref, nw_ref, sc_ref, sh_ref, g_ref, wg_ref, y_ref, gn_ref, wo_ref, o_ref, h_ref):
    def hidden():
        gate = _silu(_dot(h_ref[...], wg_ref[...]))
        heads = []
        for h0 in range(0, y_ref.shape[1], RET_V_DIM):
            cols = slice(h0, h0 + RET_V_DIM)
            y = y_ref[:, cols].astype(F32)
            yc = y - jnp.mean(y, axis=-1, keepdims=True)
            var = jnp.mean(yc * yc, axis=-1, keepdims=True)
            heads.append((yc * lax.rsqrt(var + LN_EPS)) * gn_ref[:, cols])
        return (gate * jnp.concatenate(heads, axis=1)).astype(BF16)

    _fused_steps(x_ref, nw_ref, sc_ref, sh_ref, g_ref, o_ref, h_ref, hidden, wo_ref)


def _halo_body(x_ref, nw_ref, sc_ref, sh_ref, w_ref, o_ref, *, rows_per_sample_block):
    m = x_ref.shape[0]
    if rows_per_sample_block is None:
        sc, sh = sc_ref[BATCH:BATCH + 1, :], sh_ref[BATCH:BATCH + 1, :]
    else:
        blocks = _tiles(m, rows_per_sample_block)
        pick = lambda ref: jnp.concatenate(
            [jnp.broadcast_to(ref[b:b + 1, :], (rows_per_sample_block, D_MODEL)) for b in range(blocks)], axis=0)
        sc, sh = pick(sc_ref), pick(sh_ref)
    h = _modnorm(x_ref[...], nw_ref[...] * (1.0 + sc), sh).astype(BF16)
    o_ref[...] = _dot(h, w_ref[...])


def _ffn_halo(x, mod, layer, nw, wg, *, rows_per_sample):
    rows = x.shape[0]
    tm, tn = FUSED_ROW_TILE, FUSED_COL_TILE
    nt = _tiles(rows, tm)
    t = jnp.arange(nt, dtype=jnp.int32)
    prev = jnp.maximum(t * tm - 1, 0)
    nxt = jnp.minimum((t + 1) * tm, rows - 1)
    idx = jnp.stack([prev, nxt] + [prev] * (SUBLANES - 2), axis=1).reshape(-1)
    xh = x[idx]
    m = nt * SUBLANES
    f = wg[0].shape[2]
    per_sample = None if rows_per_sample is None else SUBLANES * _tiles(rows_per_sample, tm)
    mod_slab = lambda k: pl.BlockSpec((None, SUBLANES, D_MODEL), lambda j: (layer, 0, k))
    out = pl.pallas_call(
        functools.partial(_halo_body, rows_per_sample_block=per_sample),
        out_shape=jax.ShapeDtypeStruct((m, f), F32),
        grid=(_tiles(f, tn),),
        in_specs=[
            pl.BlockSpec((m, D_MODEL), lambda j: (0, 0)),
            _slot(nw, (1, D_MODEL), lambda j: (0, 0)),
            mod_slab(4),
            mod_slab(3),
            _slot(wg, (D_MODEL, tn), lambda j: (0, j)),
        ],
        out_specs=pl.BlockSpec((m, tn), lambda j: (0, j)),
        compiler_params=_params("parallel"),
        name="ffn_halo",
    )(xh, nw[0], mod, mod, wg[0])
    return out.reshape(nt, SUBLANES, f)


def _fused_head(x, modr, layer, mod_base, nw, rows_per_sample):
    tm = FUSED_ROW_TILE
    specs = [
        pl.BlockSpec((tm, D_MODEL), _row_spec),
        _slot(nw, (1, D_MODEL), lambda i, c: (0, 0)),
        _mod_spec(layer, mod_base + 1, rows_per_sample, tm),
        _mod_spec(layer, mod_base + 0, rows_per_sample, tm),
        _mod_spec(layer, mod_base + 2, rows_per_sample, tm),
    ]
    return specs, [x, nw[0], modr, modr, modr]


def _conv_ffn(x, mod, modr, layer, nw, wg, wu, wd, dw, dwb, *, rows_per_sample, seq_len):
    rows = x.shape[0]
    tm, tf = FUSED_ROW_TILE, FUSED_COL_TILE
    f = wg[0].shape[2]
    halo = _ffn_halo(x, mod, layer, nw, wg, rows_per_sample=rows_per_sample)
    specs, args = _fused_head(x, modr, layer, 3, nw, rows_per_sample)
    specs += [
        _slot(wg, (D_MODEL, tf), lambda i, c: (0, c)),
        _slot(wu, (D_MODEL, tf), lambda i, c: (0, c)),
        _slot(wd, (tf, D_MODEL), lambda i, c: (c, 0)),
        _slot(dw, (FFN_CONV_WIDTH, tf), lambda i, c: (0, c)),
        _slot(dwb, (1, tf), lambda i, c: (0, c)),
        pl.BlockSpec((None, SUBLANES, tf), lambda i, c: (i, 0, c)),
    ]
    args += [wg[0], wu[0], wd[0], dw[0], dwb[0], halo]
    return pl.pallas_call(
        functools.partial(_ffn_body, seq_len=seq_len),
        out_shape=jax.ShapeDtypeStruct((rows, D_MODEL), F32),
        grid=(_tiles(rows, tm), _tiles(f, tf)),
        in_specs=specs,
        out_specs=pl.BlockSpec((tm, D_MODEL), _row_spec),
        scratch_shapes=[pltpu.VMEM((tm, D_MODEL), BF16)],
        compiler_params=_params("parallel", "arbitrary"),
        name="conv_ffn",
    )(*args)


def _gated_out(x, modr, layer, nw, wg, y, gn, wo, *, rows_per_sample):
    rows = x.shape[0]
    tm, tf = FUSED_ROW_TILE, GATE_COL_TILE
    f = wg[0].shape[2]
    specs, args = _fused_head(x, modr, layer, 0, nw, rows_per_sample)
    specs += [
        _slot(wg, (D_MODEL, tf), lambda i, c: (0, c)),
        pl.BlockSpec((tm, tf), lambda i, c: (i, c)),
        _slot(gn, (1, tf), lambda i, c: (0, c)),
        _slot(wo, (tf, D_MODEL), lambda i, c: (c, 0)),
    ]
    args += [wg[0], y, gn[0], wo[0]]
    return pl.pallas_call(
        _gate_body,
        out_shape=jax.ShapeDtypeStruct((rows, D_MODEL), F32),
        grid=(_tiles(rows, tm), _tiles(f, tf)),
        in_specs=specs,
        out_specs=pl.BlockSpec((tm, D_MODEL), _row_spec),
        scratch_shapes=[pltpu.VMEM((tm, D_MODEL), BF16)],
        compiler_params=_params("parallel", "arbitrary"),
        name="gated_out",
    )(*args)


def _ret_body(lg_ref, ql_ref, kl_ref, vl_ref, qc_ref, kc_ref, vc_ref, yl_ref, *rest, with_ctx):
    if with_ctx:
        yc_ref, sf_ref, sb_ref, part_ref = rest
    else:
        sf_ref, sb_ref, part_ref = rest
    head = pl.program_id(1)
    lgf = lg_ref[0, head]
    lgb = lg_ref[1, head]
    c = RET_SCAN_CHUNK
    cf = float(c)
    n = lax.broadcasted_iota(jnp.int32, (c, c), 0).astype(F32)
    m = lax.broadcasted_iota(jnp.int32, (c, c), 1).astype(F32)
    diff = n - m
    intra = (jnp.where(diff >= 0, jnp.exp(lgf * jnp.maximum(diff, 0.0)), 0.0)
             + jnp.where(diff <= 0, jnp.exp(lgb * jnp.maximum(-diff, 0.0)), 0.0))
    pv = lax.broadcasted_iota(jnp.int32, (c, RET_V_DIM), 0).astype(F32)
    pk = lax.broadcasted_iota(jnp.int32, (c, RET_QK_DIM), 0).astype(F32)
    q_dec_f = jnp.exp(lgf * (pv + 1.0))
    q_dec_b = jnp.exp(lgb * (cf - pv))
    k_dec_f = jnp.exp(lgf * (cf - 1.0 - pk))
    k_dec_b = jnp.exp(lgb * pk)
    chunk_dec_f = jnp.exp(lgf * jnp.full((1, RET_V_DIM), cf, F32))
    chunk_dec_b = jnp.exp(lgb * jnp.full((1, RET_V_DIM), cf, F32))

    def intra_term(q, k, v):
        return _dot((_dot_nt(q, k) * intra).astype(BF16), v)

    def key_state(k, v, k_dec):
        return _dot_tn((k.astype(F32) * k_dec).astype(BF16), v)

    assert CTX_LEN == c
    qc, kc, vc = qc_ref[...], kc_ref[...], vc_ref[...]
    if with_ctx:
        yc_ref[...] = intra_term(qc, kc, vc).astype(yc_ref.dtype)
    sf_ref[...] = key_state(kc, vc, k_dec_f)
    sb_ref[...] = key_state(kc, vc, k_dec_b)

    n_lat = _tiles(SEQ, c)
    half = _tiles(n_lat, 2)

    def finish(rows, y, second):
        if second:
            yl_ref[rows, :] = (y + part_ref[rows, :]).astype(yl_ref.dtype)
        else:
            part_ref[rows, :] = y

    def visit(j, second):
        rows_f = pl.ds(pl.multiple_of(j * c, c), c)
        rows_b = pl.ds(pl.multiple_of((n_lat - 1 - j) * c, c), c)
        q, k, v = ql_ref[rows_f, :], kl_ref[rows_f, :], vl_ref[rows_f, :]
        finish(rows_f, intra_term(q, k, v) + _dot(q, sf_ref[...].astype(BF16)) * q_dec_f, second)
        sf_ref[...] = sf_ref[...] * chunk_dec_f + key_state(k, v, k_dec_f)
        q, k, v = ql_ref[rows_b, :], kl_ref[rows_b, :], vl_ref[rows_b, :]
        finish(rows_b, _dot(q, sb_ref[...].astype(BF16)) * q_dec_b, second)
        sb_ref[...] = sb_ref[...] * chunk_dec_b + key_state(k, v, k_dec_b)

    def first_half(j, carry):
        visit(j, False)
        return carry

    def second_half(j, carry):
        visit(j, True)
        return carry

    lax.fori_loop(0, half, first_half, 0)
    lax.fori_loop(half, n_lat, second_half, 0)


def _retention(qkv_l, qkv_c, log_gamma, *, with_ctx):
    dk, dv, h = RET_QK_DIM, RET_V_DIM, RET_HEADS
    hv = h * dv
    k_off = h
    v_off = _tiles(2 * h * dk, dv)
    in_specs = [
        pl.BlockSpec((SEQ, dk), lambda b, n, lg: (b, n)),
        pl.BlockSpec((SEQ, dk), lambda b, n, lg: (b, k_off + n)),
        pl.BlockSpec((SEQ, dv), lambda b, n, lg: (b, v_off + n)),
        pl.BlockSpec((CTX_LEN, dk), lambda b, n, lg: (b, n)),
        pl.BlockSpec((CTX_LEN, dk), lambda b, n, lg: (b, k_off + n)),
        pl.BlockSpec((CTX_LEN, dv), lambda b, n, lg: (b, v_off + n)),
    ]
    out_shape = [jax.ShapeDtypeStruct((BATCH * SEQ, hv), BF16)]
    out_specs = [pl.BlockSpec((SEQ, dv), lambda b, n, lg: (b, n))]
    if with_ctx:
        out_shape.append(jax.ShapeDtypeStruct((BATCH * CTX_LEN, hv), BF16))
        out_specs.append(pl.BlockSpec((CTX_LEN, dv), lambda b, n, lg: (b, n)))
    outs = pl.pallas_call(
        functools.partial(_ret_body, with_ctx=with_ctx),
        out_shape=out_shape,
        grid_spec=pltpu.PrefetchScalarGridSpec(
            num_scalar_prefetch=1,
            grid=(BATCH, h),
            in_specs=in_specs,
            out_specs=out_specs,
            scratch_shapes=[pltpu.VMEM((dk, dv), F32), pltpu.VMEM((dk, dv), F32), pltpu.VMEM((SEQ, dv), F32)],
        ),
        compiler_params=_params("parallel", "parallel"),
        name="retention",
    )(log_gamma, qkv_l, qkv_l, qkv_l, qkv_c, qkv_c, qkv_c)
    return (outs[0], outs[1]) if with_ctx else (outs[0], None)


def _att_body(q_ref, kc_ref, vc_ref, *rest, with_lat):
    if with_lat:
        kl_ref, vl_ref, o_ref, vtc_ref, vtl_ref = rest
    else:
        o_ref, vtc_ref = rest
    hd = ATT_HEAD_DIM
    tq = q_ref.shape[0]
    kvc = ATT_KV_CHUNK

    @pl.when(pl.program_id(2) == 0)
    def _():
        def fill(vt_ref, v_ref, r0, n):
            vt_ref[0:hd, r0:r0 + n] = v_ref[r0:r0 + n, :].astype(F32).T.astype(BF16)
            extra = lax.broadcasted_iota(jnp.int32, (BF16_ROWS, n), 0)
            vt_ref[hd:, r0:r0 + n] = jnp.where(extra == 0, 1.0, 0.0).astype(BF16)

        fill(vtc_ref, vc_ref, 0, CTX_LEN)
        if with_lat:
            for c in range(_tiles(SEQ, kvc)):
                fill(vtl_ref, vl_ref, c * kvc, kvc)

    q_all = jnp.concatenate([q_ref[:, g * hd:(g + 1) * hd] for g in range(ATT_GROUPS)], axis=0)
    chunks = [(kc_ref, vtc_ref, 0, CTX_LEN)]
    if with_lat:
        chunks += [(kl_ref, vtl_ref, c * kvc, kvc) for c in range(_tiles(SEQ, kvc))]

    def scores(chunk):
        k_ref, _, r0, n = chunk
        return _dot_nt(k_ref[r0:r0 + n, :], q_all)

    def weighted_values(chunk, p):
        _, vt_ref, r0, n = chunk
        return _dot(vt_ref[:, r0:r0 + n], p)

    mx = acc = None
    pending = None
    s_next = scores(chunks[0])
    for idx, chunk in enumerate(chunks):
        s = s_next
        if idx + 1 < len(chunks):
            s_next = scores(chunks[idx + 1])
        if pending is not None:
            pv = weighted_values(pending[0], pending[1])
            acc = pv if acc is None else acc * pending[2] + pv
        mx_new = jnp.max(s, axis=0, keepdims=True)
        alpha = None
        if mx is not None:
            mx_new = jnp.maximum(mx, mx_new)
            alpha = jnp.exp2(mx - mx_new)
        p = jnp.exp2(s - mx_new)
        mx = mx_new
        pending = (chunk, p.astype(BF16), alpha)
    pv = weighted_values(pending[0], pending[1])
    acc = pv if acc is None else acc * pending[2] + pv
    out = acc[0:hd, :] / acc[hd:hd + 1, :]
    for g in range(ATT_GROUPS):
        o_ref[:, g * hd:(g + 1) * hd] = out[:, g * tq:(g + 1) * tq].T.astype(o_ref.dtype)


def _attention(q, kv_c, kv_l):
    hd, kvh = ATT_HEAD_DIM, ATT_KV_HEADS
    gw = ATT_GROUPS * hd
    n_q = CTX_LEN if kv_l is None else SEQ
    tq = min(ATT_Q_TILE, n_q)
    qt = _tiles(n_q, tq)
    q_spec = pl.BlockSpec((tq, gw), lambda b, n, t: (b * qt + t, n))
    specs = [q_spec,
             pl.BlockSpec((CTX_LEN, hd), lambda b, n, t: (b, n)),
             pl.BlockSpec((CTX_LEN, hd), lambda b, n, t: (b, kvh + n))]
    args = [q, kv_c, kv_c]
    scratch = [pltpu.VMEM((hd + BF16_ROWS, CTX_LEN), BF16)]
    if kv_l is not None:
        specs += [pl.BlockSpec((SEQ, hd), lambda b, n, t: (b, n)),
                  pl.BlockSpec((SEQ, hd), lambda b, n, t: (b, kvh + n))]
        args += [kv_l, kv_l]
        scratch += [pltpu.VMEM((hd + BF16_ROWS, SEQ), BF16)]
    return pl.pallas_call(
        functools.partial(_att_body, with_lat=kv_l is not None),
        out_shape=jax.ShapeDtypeStruct(q.shape, BF16),
        grid=(BATCH, kvh, qt),
        in_specs=specs,
        out_specs=q_spec,
        scratch_shapes=scratch,
        compiler_params=_params("parallel", "parallel", "arbitrary"),
        name="attention",
    )(*args)


def _dwconv_body(ap_ref, a_ref, an_ref, dw_ref, dwb_ref, lnw_ref, lnb_ref, o_ref, buf_ref, cv_ref, *, seq_tiles):
    tm = a_ref.shape[0]
    pos = pl.program_id(0) % seq_tiles
    buf_ref[0:CONV_HALO, :] = jnp.where(pos == 0, 0.0, ap_ref[...])
    buf_ref[CONV_HALO:CONV_HALO + tm, :] = a_ref[...]
    buf_ref[CONV_HALO + tm:, :] = jnp.where(pos == seq_tiles - 1, 0.0, an_ref[...])
    first = CONV_HALO - CONV_WIDTH // 2
    rows = tm + 2 * CONV_HALO
    for cc in range(_tiles(D_MODEL, CONV_COL_CHUNK)):
        cols = slice(cc * CONV_COL_CHUNK, (cc + 1) * CONV_COL_CHUNK)
        acc = jnp.broadcast_to(dwb_ref[:, cols], (tm, CONV_COL_CHUNK))
        window = buf_ref[:, cols]
        for shift in range(SUBLANES):
            taps = [d for d in range(CONV_WIDTH) if (first + d) % SUBLANES == shift]
            shifted = window if shift == 0 else pltpu.roll(window, rows - shift, 0)
            for d in taps:
                r0 = first + d - shift
                acc = acc + shifted[r0:r0 + tm, :] * dw_ref[d:d + 1, cols]
        cv_ref[:, cols] = acc
    y = cv_ref[...]
    mu = jnp.mean(y, axis=-1, keepdims=True)
    yc = y - mu
    var = jnp.mean(yc * yc, axis=-1, keepdims=True)
    z = (yc * lax.rsqrt(var + LN_EPS)) * lnw_ref[...] + lnb_ref[...]
    o_ref[...] = _silu(z).astype(o_ref.dtype)


def _dwconv_ln_silu(a, dw, dwb, lnw, lnb, *, seq_len):
    rows = a.shape[0]
    tm = CONV_ROW_TILE
    per = _tiles(tm, CONV_HALO)
    last = _tiles(rows, CONV_HALO) - 1
    vec = lambda p: _slot(p, (1, D_MODEL), lambda i: (0, 0))
    return pl.pallas_call(
        functools.partial(_dwconv_body, seq_tiles=_tiles(seq_len, tm)),
        out_shape=jax.ShapeDtypeStruct((rows, D_MODEL), BF16),
        grid=(_tiles(rows, tm),),
        in_specs=[
            pl.BlockSpec((CONV_HALO, D_MODEL), lambda i: (jnp.maximum(i * per - 1, 0), 0)),
            pl.BlockSpec((tm, D_MODEL), lambda i: (i, 0)),
            pl.BlockSpec((CONV_HALO, D_MODEL), lambda i: (jnp.minimum((i + 1) * per, last), 0)),
            _slot(dw, (CONV_WIDTH, D_MODEL), lambda i: (0, 0)),
            vec(dwb), vec(lnw), vec(lnb),
        ],
        out_specs=pl.BlockSpec((tm, D_MODEL), lambda i: (i, 0)),
        scratch_shapes=[pltpu.VMEM((tm + 2 * CONV_HALO, D_MODEL), F32), pltpu.VMEM((tm, D_MODEL), F32)],
        compiler_params=_params("parallel"),
        name="dwconv_ln_silu",
    )(a, a, a, dw[0], dwb[0], lnw[0], lnb[0])


def _rope_tables():
    nf = ATT_HEAD_DIM // 4
    t = jnp.arange(SEQ)
    inv_freq = ROPE_BASE ** (-jnp.arange(nf, dtype=F32) / nf)
    ang_r = (t // GRID_W).astype(F32)[:, None] * inv_freq[None, :]
    ang_c = (t % GRID_W).astype(F32)[:, None] * inv_freq[None, :]
    cos = jnp.concatenate([jnp.cos(ang_r)] * 2 + [jnp.cos(ang_c)] * 2, axis=-1)
    sin = jnp.concatenate([-jnp.sin(ang_r), jnp.sin(ang_r), -jnp.sin(ang_c), jnp.sin(ang_c)], axis=-1)
    return cos, sin


def _swap_pairs(v):
    quarter = ATT_HEAD_DIM // 4
    shape = v.shape
    return jnp.flip(v.reshape(shape[:-1] + (shape[-1] // (2 * quarter), 2, quarter)), axis=-2).reshape(shape)


def _rows(v):
    return v.reshape(v.shape[0], 1, v.shape[1])


def kernel(x, c, ctx, c_ctx, mod_w, mod_b, norm1_w, norm2_w, ffn_w_gate, ffn_w_up, ffn_dw, ffn_dw_b, ffn_w_down, ret_wq, ret_wk, ret_wv, ret_wg, ret_wo, ret_decay, ret_gn_w, att_wq, att_wkv, att_q_gain, att_k_gain, att_wo, cnv_w1, cnv_b1, cnv_dw, cnv_dw_b, cnv_ln_w, cnv_ln_b, cnv_w2, cnv_b2):
    xl = x.reshape(BATCH * SEQ, D_MODEL)
    xc = ctx.reshape(BATCH * CTX_LEN, D_MODEL)

    cin = jnp.concatenate([c, c_ctx[None, :], jnp.zeros((SUBLANES - BATCH - 1, D_MODEL), F32)], axis=0)
    mod = _modulation(cin, mod_w, mod_b)
    modr = mod.reshape(DEPTH, SUBLANES, N_MOD, D_MODEL).transpose(0, 2, 1, 3)[:, :, :, None, :]
    rope = _rope_tables()

    bf = lambda w: w.astype(BF16)
    norm1, norm2 = _rows(norm1_w), _rows(norm2_w)
    ffn_wg, ffn_wu, ffn_wd, ffn_b = bf(ffn_w_gate), bf(ffn_w_up), bf(ffn_w_down), _rows(ffn_dw_b)
    ret_qkv = bf(jnp.concatenate([ret_wq, ret_wk * (RET_QK_DIM ** -0.5), ret_wv], axis=-1))
    ret_g, ret_o, ret_gn = bf(ret_wg), bf(ret_wo), _rows(ret_gn_w)
    ret_lg = -jnp.exp(ret_decay.astype(F32))
    att_q, att_kv, att_o = bf(att_wq), bf(att_wkv), bf(att_wo)
    att_q_sw = _swap_pairs(att_q)
    att_k_sw = _swap_pairs(att_kv[:, :, :ATT_KV_HEADS * ATT_HEAD_DIM])
    att_qg, att_kg = _rows(att_q_gain), _rows(att_k_gain)
    cnv_1, cnv_2 = bf(cnv_w1), bf(cnv_w2)
    cnv_b1r, cnv_b2r, cnv_dwb, cnv_lnw, cnv_lnb = map(_rows, (cnv_b1, cnv_b2, cnv_dw_b, cnv_ln_w, cnv_ln_b))

    for i in range(DEPTH):
        with_ctx = i < DEPTH - 1
        kind, j = i % N_MIXERS, i // N_MIXERS
        nw1, nw2 = (norm1, i), (norm2, i)
        if kind == 0:
            qkv_l = _norm_proj(xl, modr, i, 0, nw1, (ret_qkv, j), rows_per_sample=SEQ, out_dtype=BF16)
            qkv_c = _norm_proj(xc, modr, i, 0, nw1, (ret_qkv, j), rows_per_sample=None, out_dtype=BF16)
            y_l, y_c = _retention(qkv_l, qkv_c, ret_lg[j], with_ctx=with_ctx)
            xl = _gated_out(xl, modr, i, nw1, (ret_g, j), y_l, (ret_gn, j), (ret_o, j), rows_per_sample=SEQ)
            if with_ctx:
                xc = _gated_out(xc, modr, i, nw1, (ret_g, j), y_c, (ret_gn, j), (ret_o, j), rows_per_sample=None)
        elif kind == 1:
            n_k = ATT_KV_HEADS * ATT_HEAD_DIM
            q_args = dict(out_dtype=BF16, mode="headnorm", gain=(att_qg, j), n_norm=D_MODEL, scale=ATT_Q_SCALE)
            kv_args = dict(out_dtype=BF16, mode="headnorm", gain=(att_kg, j), n_norm=n_k)
            cos, sin = rope
            rope_q = ((att_q_sw, j), cos * (att_q_gain[j] * ATT_Q_SCALE), sin * (_swap_pairs(att_q_gain[j]) * ATT_Q_SCALE))
            rope_k = ((att_k_sw, j), cos * att_k_gain[j], sin * _swap_pairs(att_k_gain[j]))
            q_l = _norm_proj(xl, modr, i, 0, nw1, (att_q, j), rows_per_sample=SEQ, rope=rope_q, **q_args)
            kv_l = _norm_proj(xl, modr, i, 0, nw1, (att_kv, j), rows_per_sample=SEQ, rope=rope_k, **kv_args)
            kv_c = _norm_proj(xc, modr, i, 0, nw1, (att_kv, j), rows_per_sample=None, **kv_args)
            y_l = _attention(q_l, kv_c, kv_l)
            xl = _matmul_resid(y_l, (att_o, j), None, xl, modr, i, 0, rows_per_sample=SEQ)
            if with_ctx:
                q_c = _norm_proj(xc, modr, i, 0, nw1, (att_q, j), rows_per_sample=None, **q_args)
                y_c = _attention(q_c, kv_c, None)
                xc = _matmul_resid(y_c, (att_o, j), None, xc, modr, i, 0, rows_per_sample=None)
        else:
            def conformer(xs, rows_per_sample, seq_len):
                a = _norm_proj(xs, modr, i, 0, nw1, (cnv_1, j), rows_per_sample=rows_per_sample, out_dtype=F32,
                               mode="glu", bias=(cnv_b1r, j))
                a = _dwconv_ln_silu(a, (cnv_dw, j), (cnv_dwb, j), (cnv_lnw, j), (cnv_lnb, j), seq_len=seq_len)
                return _matmul_resid(a, (cnv_2, j), (cnv_b2r, j), xs, modr, i, 0, rows_per_sample=rows_per_sample)

            xl = conformer(xl, SEQ, SEQ)
            if with_ctx:
                xc = conformer(xc, None, CTX_LEN)
        ffn = functools.partial(_conv_ffn, mod=mod, modr=modr, layer=i, nw=nw2, wg=(ffn_wg, i), wu=(ffn_wu, i),
                                wd=(ffn_wd, i), dw=(ffn_dw, i), dwb=(ffn_b, i))
        xl = ffn(xl, rows_per_sample=SEQ, seq_len=SEQ)
        if with_ctx:
            xc = ffn(xc, rows_per_sample=None, seq_len=CTX_LEN)
    return xl.reshape(BATCH, SEQ, D_MODEL)
```
